```python
import jax
import jax.numpy as jnp
from jax import lax

D_MODEL = 2048
BATCH = 1
SEQ = 16384
DEPTH = 2

GRID_W = 64
CTX_LEN = 256
D_HGRN = 1024
HGRN_HEADS = 8
HGRN_HEAD_DIM = D_HGRN // HGRN_HEADS
CHUNK = 64
D_POOL = 512
POOL_WINDOWS = (2, 4, 8, 16)
POOL_GROUP = D_POOL // len(POOL_WINDOWS)
D_CONV = 512
D_FF = 5632
D_IN = 5 * D_HGRN + D_POOL + 3 * D_CONV
ALPHA = (2 * DEPTH) ** 0.25
BETA = (8 * DEPTH) ** -0.25
EPS = 1e-6
F_MIN = 1e-30
F32 = jnp.float32

kernel_name = 'hybrid_hgrn2_pool_shortconv_dit'


def layer_norm(x, g=None, b=None):
    xf = x.astype(F32)
    mu = jnp.mean(xf, -1, keepdims=True)
    var = jnp.mean(jnp.square(xf - mu), -1, keepdims=True)
    y = (xf - mu) * lax.rsqrt(var + EPS)
    if g is not None:
        y = y * g.astype(F32) + b.astype(F32)
    return y.astype(x.dtype)


def modulate(x, shift, scale):
    return layer_norm(x) * (1 + scale) + shift


def dwconv3(x, w, b=None):
    y = lax.conv_general_dilated(x, w[:, None, :].astype(x.dtype), (1,), ((1, 1),),
                                 dimension_numbers=('NWC', 'WIO', 'NWC'),
                                 feature_group_count=x.shape[-1])
    return y if b is None else y + b


def chunk_gla(q, k, v, log_f, s0):
    B, T, H, dk = q.shape
    n = T // CHUNK
    def to_chunks(a):
        return a.astype(F32).reshape(B, n, CHUNK, H, a.shape[-1]).transpose(1, 0, 3, 2, 4)
    qc, kc, vc, gc = (to_chunks(a) for a in (q, k, v, log_f))
    causal = jnp.tril(jnp.ones((CHUNK, CHUNK), bool))[:, :, None]
    def step(S, inp):
        qi, ki, vi, gi = inp
        bcum = jnp.cumsum(gi, axis=-2)
        diff = bcum[..., :, None, :] - bcum[..., None, :, :]
        decay = jnp.where(causal, jnp.exp(jnp.where(causal, diff, 0.0)), 0.0)
        attn = jnp.einsum('bhtsk,bhsk->bhts', qi[..., :, None, :] * decay, ki)
        o = (jnp.einsum('bhts,bhsv->bhtv', attn, vi)
             + jnp.einsum('bhtk,bhkv->bhtv', qi * jnp.exp(bcum), S))
        b_last = bcum[..., -1:, :]
        S_new = (jnp.exp(b_last[..., 0, :])[..., None] * S
                 + jnp.einsum('bhsk,bhsv->bhkv', ki * jnp.exp(b_last - bcum), vi))
        return S_new, o
    S, o = lax.scan(step, s0, (qc, kc, vc, gc))
    o = o.transpose(1, 0, 3, 2, 4).reshape(B, T, H, v.shape[-1])
    return o, S


def forget_gate(z, lb):
    z = z.astype(F32)
    f = lb + (1.0 - lb) * jax.nn.sigmoid(z)
    log_f = jnp.log(jnp.maximum(f, F_MIN))
    k = (1.0 - lb) * jax.nn.sigmoid(-z)
    return log_f, k


def hgrn2_bidir(p, lb_f, lb_b, norm_w, s0_f, s0_b, with_output):
    B, T, _ = p.shape
    def heads(a):
        return a.reshape(B, T, HGRN_HEADS, HGRN_HEAD_DIM)
    q = heads(jax.nn.silu(p[..., :D_HGRN].astype(F32)))
    i = heads(p[..., D_HGRN:2 * D_HGRN])
    lf_f, k_f = forget_gate(p[..., 2 * D_HGRN:3 * D_HGRN], lb_f)
    lf_b, k_b = forget_gate(p[..., 3 * D_HGRN:4 * D_HGRN], lb_b)
    lf_f, k_f, lf_b, k_b = heads(lf_f), heads(k_f), heads(lf_b), heads(k_b)
    def flip(a):
        return jnp.flip(a, 1)
    o_f, s_f = chunk_gla(q, k_f, i, lf_f, s0_f)
    o_b, s_b = chunk_gla(flip(q), flip(k_b), flip(i), flip(lf_b), s0_b)
    if not with_output:
        return None, s_f, s_b
    o = o_f + flip(o_b)
    o = o * lax.rsqrt(jnp.mean(jnp.square(o), -1, keepdims=True) + EPS)
    o = o * norm_w.astype(F32).reshape(HGRN_HEADS, HGRN_HEAD_DIM)
    g = p[..., 4 * D_HGRN:5 * D_HGRN].astype(F32)
    o = o.reshape(B, T, D_HGRN) * jax.nn.silu(g)
    return o.astype(p.dtype), s_f, s_b


def pool_mixer(v, pool_w, pool_scale):
    N, T, _ = v.shape
    vf = v.astype(F32)
    cs = jnp.pad(jnp.cumsum(vf, 1), ((0, 0), (1, 0), (0, 0)))
    t = jnp.arange(T)
    means = []
    for gi, w in enumerate(POOL_WINDOWS):
        lo = jnp.maximum(t - w // 2, 0)
        hi = jnp.minimum(t + w // 2, T)
        csg = cs[..., gi * POOL_GROUP:(gi + 1) * POOL_GROUP]
        cnt = (hi - lo).astype(F32)[:, None]
        means.append((jnp.take(csg, hi, 1) - jnp.take(csg, lo, 1)) / cnt)
    d = jnp.stack(means, 2) - vf.reshape(N, T, len(POOL_WINDOWS), POOL_GROUP)
    y = jnp.einsum('ntgp,gpq->ntgq', d, pool_w.astype(F32)).reshape(N, T, D_POOL)
    return (y * pool_scale.astype(F32)).astype(v.dtype)


def local_mixers(p_loc, pool_w, pool_scale, conv_w):
    v = p_loc[..., :D_POOL]
    bg, cg, h = jnp.split(p_loc[..., D_POOL:], 3, -1)
    y_conv = bg * dwconv3(cg * h, conv_w)
    return jnp.concatenate([pool_mixer(v, pool_w, pool_scale), y_conv], -1)


def conv_ffn(h, w_up, conv_w, conv_b, w_down):
    u, g = jnp.split(h @ w_up, 2, -1)
    return (u * jax.nn.silu(dwconv3(g, conv_w, conv_b))) @ w_down


def setup_inputs(seed: int = 0):
    key = jax.random.key(seed)
    ks = iter(jax.random.split(key, 24))
    def nrm(shape, s):
        return s * jax.random.normal(next(ks), shape, F32)
    D = D_MODEL
    return {
        'x': nrm((BATCH, SEQ, D), 1.0),
        'c': nrm((BATCH, D), 1.0),
        'ctx': nrm((BATCH, CTX_LEN, D), 1.0),
        'c_ctx': nrm((D,), 1.0),
        'w_ada': nrm((DEPTH, D, 6 * D), 0.5 * D ** -0.5),
        'b_ada': nrm((DEPTH, 6 * D), 0.01),
        'w_in': nrm((DEPTH, D, D_IN), D ** -0.5),
        'lb_logits': nrm((2, DEPTH, D_HGRN), 0.5),
        'hgrn_norm_w': 1.0 + nrm((DEPTH, D_HGRN), 0.02),
        'pool_w': nrm((DEPTH, len(POOL_WINDOWS), POOL_GROUP, POOL_GROUP), POOL_GROUP ** -0.5),
        'pool_scale': 1.0 + nrm((DEPTH, D_POOL), 0.1),
        'conv_w': nrm((DEPTH, 3, D_CONV), 3 ** -0.5),
        'w_out': nrm((DEPTH, D, D), BETA * D ** -0.5),
        'ln1_g': 1.0 + nrm((DEPTH, D), 0.02),
        'ln1_b': nrm((DEPTH, D), 0.02),
        'w_up': nrm((DEPTH, D, 2 * D_FF), D ** -0.5),
        'ffn_conv_w': nrm((DEPTH, 3, D_FF), 3 ** -0.5),
        'ffn_conv_b': nrm((DEPTH, D_FF), 0.01),
        'w_down': nrm((DEPTH, D_FF, D), BETA * D_FF ** -0.5),
        'ln2_g': 1.0 + nrm((DEPTH, D), 0.02),
        'ln2_b': nrm((DEPTH, D), 0.02),
    }


def reference(x, c, ctx, c_ctx, w_ada, b_ada, w_in, lb_logits, hgrn_norm_w, pool_w,
              pool_scale, conv_w, w_out, ln1_g, ln1_b, w_up, ffn_conv_w, ffn_conv_b,
              w_down, ln2_g, ln2_b):
    B, L, _ = x.shape
    rows = L // GRID_W
    def to_rows(a):
        return a.reshape(B * rows, GRID_W, a.shape[-1])
    def from_rows(a):
        return a.reshape(B, L, a.shape[-1])
    p_lb = jax.nn.softmax(lb_logits.astype(F32), axis=1)
    lower = jnp.cumsum(p_lb, 1) - p_lb[:, :1]
    zero_state = jnp.zeros((B, HGRN_HEADS, HGRN_HEAD_DIM, HGRN_HEAD_DIM), F32)
    for l in range(DEPTH):
        ctx_live = l < DEPTH - 1
        mod_x = jnp.split(jax.nn.silu(c) @ w_ada[l] + b_ada[l], 6, -1)
        mod_c = jnp.split(jax.nn.silu(c_ctx) @ w_ada[l] + b_ada[l], 6, -1)
        sh1, sc1, g1, sh2, sc2, g2 = (m[:, None, :] for m in mod_x)
        csh1, csc1, cg1, csh2, csc2, cg2 = mod_c
        hx = modulate(x, sh1, sc1)
        hc = modulate(ctx, csh1, csc1)
        px = hx @ w_in[l]
        pc = hc @ w_in[l, :, :(D_IN if ctx_live else 4 * D_HGRN)]
        oc, s_f, s_b = hgrn2_bidir(pc, lower[0, l], lower[1, l], hgrn_norm_w[l],
                                   zero_state, zero_state, ctx_live)
        ox, _, _ = hgrn2_bidir(px, lower[0, l], lower[1, l], hgrn_norm_w[l], s_f, s_b, True)
        loc_x = from_rows(local_mixers(to_rows(px[..., 5 * D_HGRN:]), pool_w[l],
                                       pool_scale[l], conv_w[l]))
        yx = jnp.concatenate([ox, loc_x], -1) @ w_out[l]
        x = layer_norm(ALPHA * x + g1 * yx, ln1_g[l], ln1_b[l])
        if ctx_live:
            loc_c = local_mixers(pc[..., 5 * D_HGRN:], pool_w[l], pool_scale[l], conv_w[l])
            yc = jnp.concatenate([oc, loc_c], -1) @ w_out[l]
            ctx = layer_norm(ALPHA * ctx + cg1 * yc, ln1_g[l], ln1_b[l])
        hx = modulate(x, sh2, sc2)
        fx = from_rows(conv_ffn(to_rows(hx), w_up[l], ffn_conv_w[l], ffn_conv_b[l], w_down[l]))
        x = layer_norm(ALPHA * x + g2 * fx, ln2_g[l], ln2_b[l])
        if ctx_live:
            hc = modulate(ctx, csh2, csc2)
            fc = conv_ffn(hc, w_up[l], ffn_conv_w[l], ffn_conv_b[l], w_down[l])
            ctx = layer_norm(ALPHA * ctx + cg2 * fc, ln2_g[l], ln2_b[l])
    return x
```

```python
import functools

import jax
import jax.numpy as jnp
import numpy as np
from jax import lax
from jax.experimental import pallas as pl
from jax.experimental.pallas import tpu as pltpu

D_MODEL = 2048
DEPTH = 2
GRID_W = 64
CTX_LEN = 256
D_HGRN = 1024
HGRN_HEADS = 8
HEAD_DIM = D_HGRN // HGRN_HEADS
D_POOL = 512
POOL_WINDOWS = (2, 4, 8, 16)
POOL_GROUP = D_POOL // len(POOL_WINDOWS)
D_CONV = 512
D_FF = 5632
D_IN = 5 * D_HGRN + D_POOL + 3 * D_CONV
ALPHA = (2 * DEPTH) ** 0.25
EPS = 1e-6
F_MIN = 1e-30
F32 = jnp.float32
BF16 = jnp.bfloat16

GLA_CHUNK = 64
GLA_LEVELS = (32, 16, 8, 4, 2, 1)
MIX_ROWS = 256
ADA_ROWS = 8
VMEM_LIMIT = 56 * 1024 * 1024


def _cparams(sem):
    return pltpu.CompilerParams(dimension_semantics=sem, vmem_limit_bytes=VMEM_LIMIT)


def _sigmoid_pair(z):
    e = jnp.exp(-jnp.abs(z))
    r = 1.0 / (1.0 + e)
    er = e * r
    pos = z >= 0
    return jnp.where(pos, r, er), jnp.where(pos, er, r)


def _silu(z):
    return z * _sigmoid_pair(z)[0]


def _layer_norm(x):
    mu = jnp.mean(x, -1, keepdims=True)
    xc = x - mu
    var = jnp.mean(xc * xc, -1, keepdims=True)
    return xc * lax.rsqrt(var + EPS)


def _dot_nt(a, b):
    return lax.dot_general(a, b, (((1,), (1,)), ((), ())), preferred_element_type=F32)


def _dot(a, b):
    return jnp.dot(a, b, preferred_element_type=F32)


def _split2(a):
    hi = a.astype(BF16)
    mid = (a - hi.astype(F32)).astype(BF16)
    return hi, mid


def _ada_kernel(cc_ref, w_ref, b_ref, o_ref):
    a = _silu(cc_ref[...])
    o_ref[...] = jnp.dot(a, w_ref[...], preferred_element_type=F32,
                         precision=lax.Precision.HIGHEST) + b_ref[...]


def _ada(cc, w_ada, b_ada):
    depth, d, n = w_ada.shape
    tn = 1024
    return pl.pallas_call(
        _ada_kernel,
        grid=(depth, n // tn),
        in_specs=[
            pl.BlockSpec((ADA_ROWS, d), lambda l, j: (0, 0)),
            pl.BlockSpec((None, d, tn), lambda l, j: (l, 0, j)),
            pl.BlockSpec((None, 1, tn), lambda l, j: (l, 0, j)),
        ],
        out_specs=pl.BlockSpec((None, ADA_ROWS, tn), lambda l, j: (l, 0, j)),
        out_shape=jax.ShapeDtypeStruct((depth, ADA_ROWS, n), F32),
        compiler_params=_cparams(("arbitrary", "arbitrary")),
        name="ada",
    )(cc, w_ada, b_ada.reshape(depth, 1, n))


def _in_proj_kernel(x_ref, sh_ref, sc_ref, w_ref, o_ref, h_ref, *, mod_row):
    @pl.when(pl.program_id(1) == 0)
    def _():
        sh = sh_ref[mod_row:mod_row + 1, :]
        sc = sc_ref[mod_row:mod_row + 1, :]
        h_ref[...] = (_layer_norm(x_ref[...]) * (1.0 + sc) + sh).astype(BF16)

    o_ref[...] = _dot(h_ref[...], w_ref[...])


def _in_proj(x, mods, w, *, mod_row, tm, tn):
    m, d = x.shape
    n = w.shape[1]
    return pl.pallas_call(
        functools.partial(_in_proj_kernel, mod_row=mod_row),
        grid=(m // tm, n // tn),
        in_specs=[
            pl.BlockSpec((tm, d), lambda i, j: (i, 0)),
            pl.BlockSpec((ADA_ROWS, d), lambda i, j: (0, 0)),
            pl.BlockSpec((ADA_ROWS, d), lambda i, j: (0, 1)),
            pl.BlockSpec((d, tn), lambda i, j: (0, j)),
        ],
        out_specs=pl.BlockSpec((tm, tn), lambda i, j: (i, j)),
        out_shape=jax.ShapeDtypeStruct((m, n), F32),
        scratch_shapes=[pltpu.VMEM((tm, d), BF16)],
        compiler_params=_cparams(("arbitrary", "arbitrary")),
        name="in_proj",
    )(x, mods, mods, w)


def _gla_constants():
    n = GLA_CHUNK
    t = np.arange(n)[:, None]
    r = np.arange(n)[None, :]
    out = []
    for fwd in (True, False):
        blocks = [(r <= t) if fwd else (r >= t), (r > t) if fwd else (r < t)]
        masks = [np.eye(n)]
        for c in GLA_LEVELS:
            mid = (t // (2 * c)) * (2 * c) + c
            mid_s = (r // (2 * c)) * (2 * c) + c
            same = (t // (2 * c)) == (r // (2 * c))
            if fwd:
                late = (t >= mid) & (r >= mid) & (r <= t)
                early = (t < mid) & (r > t) & (r <= mid - 1)
                masks.append(same & (t >= mid) & (r < mid_s))
            else:
                late = (t >= mid) & (r >= mid) & (r <= t - 1)
                early = (t < mid) & (r >= t) & (r <= mid - 1)
                masks.append(same & (t < mid) & (r >= mid_s))
            blocks.append(late.astype(np.float32) - early.astype(np.float32))
        w = np.concatenate([np.asarray(b, np.float32) for b in blocks], 0)
        out.append((jnp.asarray(w, BF16), jnp.asarray(np.stack(masks), F32)))
    return out


def _lower_bound(logits, layer):
    rows = [logits[j:j + 1, :] for j in range(DEPTH)]
    mx = functools.reduce(jnp.maximum, rows)
    es = [jnp.exp(r - mx) for r in rows]
    tot = functools.reduce(jnp.add, es)
    acc = jnp.zeros_like(mx)
    for j in range(1, layer + 1):
        acc = acc + es[j] / tot
    return acc


def _gla_chunk(q_ref, v_ref, z_ref, o_ref, st_ref, w_ref, m_ref, lb, c, end_row):
    n = GLA_CHUNK
    rows = pl.ds(pl.multiple_of(c * n, n), n)
    z = z_ref[rows, :]
    v = v_ref[rows, :]
    q = _silu(q_ref[rows, :])
    sz, snz = _sigmoid_pair(z)
    g = jnp.log(jnp.maximum(lb + (1.0 - lb) * sz, F_MIN))
    k = (1.0 - lb) * snz

    g_hi, g_mid = _split2(g)
    w = w_ref[...]
    d = _dot(w, g_hi) + _dot(w, g_mid)
    b = d[0:n]
    q_in = q * jnp.exp(b)
    k_out = k * jnp.exp(d[n:2 * n])
    d_all = jnp.exp(b[end_row:end_row + 1, :])

    a = m_ref[0] * _dot_nt(q.astype(BF16), k.astype(BF16))
    for l in range(len(GLA_LEVELS)):
        e = jnp.exp(-jnp.abs(d[(2 + l) * n:(3 + l) * n]))
        a = a + m_ref[1 + l] * _dot_nt((q * e).astype(BF16), (k * e).astype(BF16))

    st = st_ref[...]
    o_ref[rows, :] = _dot(a.astype(BF16), v.astype(BF16)) + _dot_nt(q_in.astype(BF16), st.astype(BF16))
    st_ref[...] = d_all * st + _dot(v.T.astype(BF16), k_out.astype(BF16))


def _gla_kernel(qf_ref, vf_ref, zf_ref, qb_ref, vb_ref, zb_ref, lb_ref, s0f_ref, s0b_ref,
                wf_ref, mf_ref, wb_ref, mb_ref,
                of_ref, ob_ref, sf_ref, sb_ref, stf, stb, *, layer, nchunks):
    step = pl.program_id(1)

    @pl.when(step == 0)
    def _():
        stf[...] = s0f_ref[...]
        stb[...] = s0b_ref[...]

    lbf = _lower_bound(lb_ref[0], layer)
    lbb = _lower_bound(lb_ref[1], layer)

    def body(ci, carry):
        _gla_chunk(qf_ref, vf_ref, zf_ref, of_ref, stf, wf_ref, mf_ref, lbf, ci, GLA_CHUNK - 1)
        _gla_chunk(qb_ref, vb_ref, zb_ref, ob_ref, stb, wb_ref, mb_ref, lbb, nchunks - 1 - ci, 0)
        return carry

    lax.fori_loop(0, nchunks, body, 0)

    @pl.when(step == pl.num_programs(1) - 1)
    def _():
        sf_ref[...] = stf[...]
        sb_ref[...] = stb[...]


def _gla(p, lb_logits, s0f, s0b, consts, *, layer, tb):
    t = p.shape[0]
    nb = t // tb
    hd, h = HEAD_DIM, HGRN_HEADS
    (wf, mf), (wb, mb) = consts

    def col(group, rev):
        if rev:
            return pl.BlockSpec((tb, hd), lambda hh, n: (nb - 1 - n, group * h + hh))
        return pl.BlockSpec((tb, hd), lambda hh, n: (n, group * h + hh))

    state_spec = pl.BlockSpec((None, hd, hd), lambda hh, n: (hh, 0, 0))
    const2 = lambda a: pl.BlockSpec(a.shape, lambda hh, n: (0,) * a.ndim)
    return pl.pallas_call(
        functools.partial(_gla_kernel, layer=layer, nchunks=tb // GLA_CHUNK),
        grid=(h, nb),
        in_specs=[
            col(0, False), col(1, False), col(2, False),
            col(0, True), col(1, True), col(3, True),
            pl.BlockSpec((2, DEPTH, hd), lambda hh, n: (0, 0, hh)),
            state_spec, state_spec,
            const2(wf), const2(mf), const2(wb), const2(mb),
        ],
        out_specs=[
            pl.BlockSpec((tb, hd), lambda hh, n: (n, hh)),
            pl.BlockSpec((tb, hd), lambda hh, n: (nb - 1 - n, hh)),
            state_spec, state_spec,
        ],
        out_shape=[
            jax.ShapeDtypeStruct((t, D_HGRN), F32),
            jax.ShapeDtypeStruct((t, D_HGRN), F32),
            jax.ShapeDtypeStruct((h, hd, hd), F32),
            jax.ShapeDtypeStruct((h, hd, hd), F32),
        ],
        scratch_shapes=[pltpu.VMEM((hd, hd), F32), pltpu.VMEM((hd, hd), F32)],
        compiler_params=_cparams(("arbitrary", "arbitrary")),
        name="gla",
    )(p, p, p, p, p, p, lb_logits, s0f, s0b, wf, mf, wb, mb)


def _pool_constants(period):
    n = MIX_ROWS
    t = np.arange(n)
    tau = t % period
    base = t - tau
    bands, cnts = [], []
    for w in POOL_WINDOWS:
        lo = np.maximum(tau - w // 2, 0)
        hi = np.minimum(tau + w // 2, period)
        s = np.arange(n)[None, :]
        bands.append(((s >= (base + lo)[:, None]) & (s < (base + hi)[:, None])).astype(np.float32))
        cnts.append(np.broadcast_to((hi - lo).astype(np.float32)[:, None], (n, POOL_GROUP)))
    return jnp.asarray(np.stack(bands), BF16), jnp.asarray(np.stack(cnts), F32)


def _row_edge_masks(nrows, period):
    row = lax.broadcasted_iota(jnp.int32, (nrows, 1), 0)
    tau = row & (period - 1)
    return tau != 0, tau != period - 1


def _dwconv3(x, w_ref, has_left, has_right):
    nrows = x.shape[0]
    left = jnp.where(has_left, pltpu.roll(x, 1, 0), 0.0)
    right = jnp.where(has_right, pltpu.roll(x, nrows - 1, 0), 0.0)
    return w_ref[0:1, :] * left + w_ref[1:2, :] * x + w_ref[2:3, :] * right


def _mix_out_kernel(x_ref, g_ref, vb_ref, ch_ref, of_ref, ob_ref, nw_ref, pw_ref, ps_ref, cw_ref,
                    band_ref, cnt_ref, wout_ref, g1_ref, lng_ref, lnb_ref, o_ref, cat_ref,
                    *, mod_row, period, tm):
    has_left, has_right = _row_edge_masks(MIX_ROWS, period)
    g1 = g1_ref[mod_row:mod_row + 1, :]
    for sb in range(tm // MIX_ROWS):
        rows = slice(sb * MIX_ROWS, (sb + 1) * MIX_ROWS)
        for h in range(HGRN_HEADS):
            cols = slice(h * HEAD_DIM, (h + 1) * HEAD_DIM)
            o = of_ref[rows, cols] + ob_ref[rows, cols]
            o = o * lax.rsqrt(jnp.mean(o * o, -1, keepdims=True) + EPS) * nw_ref[:, cols]
            cat_ref[rows, cols] = (o * _silu(g_ref[rows, cols])).astype(BF16)
        for gi in range(len(POOL_WINDOWS)):
            cols = slice(gi * POOL_GROUP, (gi + 1) * POOL_GROUP)
            v = vb_ref[rows, cols]
            v_hi, v_mid = _split2(v)
            band = band_ref[gi]
            dlt = (_dot(band, v_hi) + _dot(band, v_mid)) / cnt_ref[gi] - v
            y = _dot(dlt.astype(BF16), pw_ref[gi]) * ps_ref[:, cols]
            cat_ref[rows, D_HGRN + gi * POOL_GROUP:D_HGRN + (gi + 1) * POOL_GROUP] = y.astype(BF16)
        bg = vb_ref[rows, D_POOL:]
        ch = ch_ref[rows, :D_CONV] * ch_ref[rows, D_CONV:]
        cat_ref[rows, D_HGRN + D_POOL:] = (bg * _dwconv3(ch, cw_ref, has_left, has_right)).astype(BF16)

        y = _dot(cat_ref[rows, :], wout_ref[...])
        r = ALPHA * x_ref[rows, :] + g1 * y
        o_ref[rows, :] = _layer_norm(r) * lng_ref[...] + lnb_ref[...]


def _mix_out(x, p, o_f, o_b, mods, norm_w, pool_w, pool_scale, conv_w, pool_consts, w_out,
             ln_g, ln_b, *, mod_row, period, tm):
    m, d = x.shape
    band, cnt = pool_consts
    const = lambda a: pl.BlockSpec(a.shape, lambda i: (0,) * a.ndim)
    row = lambda width, cb: pl.BlockSpec((tm, width), lambda i: (i, cb))
    vec = lambda a: a.reshape(1, -1)
    args = [
        (x, row(d, 0)),
        (p, row(D_HGRN, 4)),
        (p, row(D_POOL + D_CONV, 5)),
        (p, row(2 * D_CONV, 6)),
        (o_f, row(D_HGRN, 0)),
        (o_b, row(D_HGRN, 0)),
        (vec(norm_w), None), (pool_w, None), (vec(pool_scale), None), (conv_w, None),
        (band, None), (cnt, None), (w_out, None),
        (mods, pl.BlockSpec((ADA_ROWS, d), lambda i: (0, 2))),
        (vec(ln_g), None), (vec(ln_b), None),
    ]
    return pl.pallas_call(
        functools.partial(_mix_out_kernel, mod_row=mod_row, period=period, tm=tm),
        grid=(m // tm,),
        in_specs=[s if s is not None else const(a) for a, s in args],
        out_specs=pl.BlockSpec((tm, d), lambda i: (i, 0)),
        out_shape=jax.ShapeDtypeStruct((m, d), F32),
        scratch_shapes=[pltpu.VMEM((tm, d), BF16)],
        compiler_params=_cparams(("arbitrary",)),
        name="mix_out",
    )(*[a for a, _ in args])


def _ffn_kernel(x_ref, sh_ref, sc_ref, g2_ref, wu_ref, wg_ref, cw_ref, cb_ref, wd_ref,
                lng_ref, lnb_ref, o_ref, h_ref, *, mod_row, period, tm):
    j = pl.program_id(1)

    @pl.when(j == 0)
    def _():
        sh = sh_ref[mod_row:mod_row + 1, :]
        sc = sc_ref[mod_row:mod_row + 1, :]
        h_ref[...] = (_layer_norm(x_ref[...]) * (1.0 + sc) + sh).astype(BF16)

    has_left, has_right = _row_edge_masks(tm, period)
    h = h_ref[...]
    u = _dot(h, wu_ref[...])
    g = _dot(h, wg_ref[...])
    act = u * _silu(_dwconv3(g, cw_ref, has_left, has_right) + cb_ref[...])
    part = _dot(act.astype(BF16), wd_ref[...])

    @pl.when(j == 0)
    def _():
        o_ref[...] = part

    @pl.when(j > 0)
    def _():
        o_ref[...] += part

    @pl.when(j == pl.num_programs(1) - 1)
    def _():
        r = ALPHA * x_ref[...] + g2_ref[mod_row:mod_row + 1, :] * o_ref[...]
        o_ref[...] = _layer_norm(r) * lng_ref[...] + lnb_ref[...]


def _ffn(x, mods, w_up, conv_w, conv_b, w_down, ln_g, ln_b, *, mod_row, period, tm, tf):
    m, d = x.shape
    nf = D_FF // tf
    vec = lambda a: a.reshape(1, -1)
    mod = lambda cb: pl.BlockSpec((ADA_ROWS, d), lambda i, j: (0, cb))
    const = lambda a: pl.BlockSpec(a.shape, lambda i, j: (0,) * a.ndim)
    return pl.pallas_call(
        functools.partial(_ffn_kernel, mod_row=mod_row, period=period, tm=tm),
        grid=(m // tm, nf),
        in_specs=[
            pl.BlockSpec((tm, d), lambda i, j: (i, 0)),
            mod(3), mod(4), mod(5),
            pl.BlockSpec((d, tf), lambda i, j: (0, j)),
            pl.BlockSpec((d, tf), lambda i, j: (0, nf + j)),
            pl.BlockSpec((3, tf), lambda i, j: (0, j)),
            pl.BlockSpec((1, tf), lambda i, j: (0, j)),
            pl.BlockSpec((tf, d), lambda i, j: (j, 0)),
            const(vec(ln_g)), const(vec(ln_b)),
        ],
        out_specs=pl.BlockSpec((tm, d), lambda i, j: (i, 0)),
        out_shape=jax.ShapeDtypeStruct((m, d), F32),
        scratch_shapes=[pltpu.VMEM((tm, d), BF16)],
        compiler_params=_cparams(("arbitrary", "arbitrary")),
        name="ffn",
    )(x, mods, mods, mods, w_up, w_up, conv_w, vec(conv_b), w_down, vec(ln_g), vec(ln_b))


def kernel(x, c, ctx, c_ctx, w_ada, b_ada, w_in, lb_logits, hgrn_norm_w, pool_w, pool_scale, conv_w,
           w_out, ln1_g, ln1_b, w_up, ffn_conv_w, ffn_conv_b, w_down, ln2_g, ln2_b):
    assert x.shape[0] == 1 and ctx.shape[0] == 1, "batch size 1 only"
    xs, cs = x[0], ctx[0]
    seq, ctx_len = xs.shape[0], cs.shape[0]

    cc = jnp.concatenate([c, c_ctx[None, :], jnp.zeros((ADA_ROWS - 2, D_MODEL), F32)], 0)
    mods_all = _ada(cc, w_ada, b_ada)

    gla_consts = _gla_constants()
    pool_x = _pool_constants(GRID_W)
    pool_c = _pool_constants(ctx_len)
    zero_state = jnp.zeros((HGRN_HEADS, HEAD_DIM, HEAD_DIM), F32)
    w_in_b, w_out_b = w_in.astype(BF16), w_out.astype(BF16)
    w_up_b, w_down_b, pool_w_b = w_up.astype(BF16), w_down.astype(BF16), pool_w.astype(BF16)

    for l in range(DEPTH):
        ctx_live = l < DEPTH - 1
        mods = mods_all[l]
        pc = _in_proj(cs, mods, w_in_b[l], mod_row=1, tm=ctx_len, tn=1024)
        px = _in_proj(xs, mods, w_in_b[l], mod_row=0, tm=1024, tn=1024)
        ocf, ocb, s_f, s_b = _gla(pc, lb_logits, zero_state, zero_state, gla_consts, layer=l, tb=ctx_len)
        oxf, oxb, _, _ = _gla(px, lb_logits, s_f, s_b, gla_consts, layer=l, tb=1024)
        mix = functools.partial(_mix_out, mods=mods, norm_w=hgrn_norm_w[l], pool_w=pool_w_b[l],
                                pool_scale=pool_scale[l], conv_w=conv_w[l], w_out=w_out_b[l],
                                ln_g=ln1_g[l], ln_b=ln1_b[l], tm=MIX_ROWS)
        ffn = functools.partial(_ffn, mods=mods, w_up=w_up_b[l], conv_w=ffn_conv_w[l],
                                conv_b=ffn_conv_b[l], w_down=w_down_b[l], ln_g=ln2_g[l], ln_b=ln2_b[l],
                                tf=512)
        xs = mix(xs, px, oxf, oxb, pool_consts=pool_x, mod_row=0, period=GRID_W)
        if ctx_live:
            cs = mix(cs, pc, ocf, ocb, pool_consts=pool_c, mod_row=1, period=ctx_len)
        xs = ffn(xs, mod_row=0, period=GRID_W, tm=512)
        if ctx_live:
            cs = ffn(cs, mod_row=1, period=ctx_len, tm=ctx_len)
    return xs[None]
```

```python
import functools

import jax
import jax.numpy as jnp
import numpy as np
from jax import lax
from jax.experimental import pallas as pl
from jax.experimental.pallas import tpu as pltpu

D_MODEL = 2048
DEPTH = 2
GRID_W = 64
CTX_LEN = 256
D_HGRN = 1024
HGRN_HEADS = 8
HEAD_DIM = D_HGRN // HGRN_HEADS
D_POOL = 512
POOL_WINDOWS = (2, 4, 8, 16)
POOL_GROUP = D_POOL // len(POOL_WINDOWS)
D_CONV = 512
D_FF = 5632
D_IN = 5 * D_HGRN + D_POOL + 3 * D_CONV
ALPHA = (2 * DEPTH) ** 0.25
EPS = 1e-6
F_MIN = 1e-30
F32 = jnp.float32
BF16 = jnp.bfloat16

GLA_CHUNK = 64
GLA_LEVELS = (32, 16, 8, 4, 2, 1)
GLA_UNROLL = 4
GLA_SKEW = 3
MIX_ROWS = 256
FFN_SPLIT = 2
ADA_ROWS = 8
VMEM_LIMIT = 56 * 1024 * 1024


def _cparams(sem):
    return pltpu.CompilerParams(dimension_semantics=sem, vmem_limit_bytes=VMEM_LIMIT)


def _silu(z):
    h = 0.5 * z
    return h + h * jnp.tanh(h)


def _layer_norm(x):
    mu = jnp.mean(x, -1, keepdims=True)
    xc = x - mu
    var = jnp.mean(xc * xc, -1, keepdims=True)
    return xc * lax.rsqrt(var + EPS)


def _dot_nt(a, b):
    return lax.dot_general(a, b, (((1,), (1,)), ((), ())), preferred_element_type=F32)


def _dot(a, b):
    return jnp.dot(a, b, preferred_element_type=F32)


def _split2(a):
    hi = a.astype(BF16)
    mid = (a - hi.astype(F32)).astype(BF16)
    return hi, mid


def _ada_kernel(cc_ref, w_ref, b_ref, o_ref):
    a = _silu(cc_ref[...])
    o_ref[...] = jnp.dot(a, w_ref[...], preferred_element_type=F32,
                         precision=lax.Precision.HIGHEST) + b_ref[...]


def _ada(cc, w_ada, b_ada):
    depth, d, n = w_ada.shape
    tn = 1024
    return pl.pallas_call(
        _ada_kernel,
        grid=(depth, n // tn),
        in_specs=[
            pl.BlockSpec((ADA_ROWS, d), lambda l, j: (0, 0)),
            pl.BlockSpec((None, d, tn), lambda l, j: (l, 0, j)),
            pl.BlockSpec((None, 1, tn), lambda l, j: (l, 0, j)),
        ],
        out_specs=pl.BlockSpec((None, ADA_ROWS, tn), lambda l, j: (l, 0, j)),
        out_shape=jax.ShapeDtypeStruct((depth, ADA_ROWS, n), F32),
        compiler_params=_cparams(("arbitrary", "arbitrary")),
        name="ada",
    )(cc, w_ada, b_ada.reshape(depth, 1, n))


def _in_proj_kernel(x_ref, sh_ref, sc_ref, w_ref, o_ref, h_ref, *, mod_row):
    @pl.when(pl.program_id(1) == 0)
    def _():
        sh = sh_ref[mod_row:mod_row + 1, :]
        sc = sc_ref[mod_row:mod_row + 1, :]
        h_ref[...] = (_layer_norm(x_ref[...]) * (1.0 + sc) + sh).astype(BF16)

    o_ref[...] = _dot(h_ref[...], w_ref[...])


def _in_proj(x, mods, w, *, layer, mod_row, tm, tn):
    m, d = x.shape
    n = w.shape[2]
    return pl.pallas_call(
        functools.partial(_in_proj_kernel, mod_row=mod_row),
        grid=(m // tm, n // tn),
        in_specs=[
            pl.BlockSpec((tm, d), lambda i, j: (i, 0)),
            pl.BlockSpec((ADA_ROWS, d), lambda i, j: (0, 0)),
            pl.BlockSpec((ADA_ROWS, d), lambda i, j: (0, 1)),
            pl.BlockSpec((None, d, tn), lambda i, j: (layer, 0, j)),
        ],
        out_specs=pl.BlockSpec((tm, tn), lambda i, j: (i, j)),
        out_shape=jax.ShapeDtypeStruct((m, n), F32),
        scratch_shapes=[pltpu.VMEM((tm, d), BF16)],
        compiler_params=_cparams(("arbitrary", "arbitrary")),
        name="in_proj",
    )(x, mods, mods, w)


def _gla_constants():
    n = GLA_CHUNK
    t = np.arange(n)[:, None]
    r = np.arange(n)[None, :]
    out = []
    for fwd in (True, False):
        masks = []
        for c in GLA_LEVELS:
            mid = (t // (2 * c)) * (2 * c) + c
            mid_s = (r // (2 * c)) * (2 * c) + c
            same = (t // (2 * c)) == (r // (2 * c))
            masks.append(same & ((t >= mid) & (r < mid_s) if fwd else (t < mid) & (r >= mid_s)))
        mid = (t // 4) * 4 + 2
        if fwd:
            cum = r <= t
            lvl2 = ((t >= mid) & (r >= mid) & (r <= t)) | ((t < mid) & (r > t) & (r <= mid - 1))
        else:
            cum = r >= t
            lvl2 = ((t >= mid) & (r >= mid) & (r <= t - 1)) | ((t < mid) & (r >= t) & (r <= mid - 1))
        w = np.concatenate([cum, lvl2], 0).astype(np.float32)
        out.append((jnp.asarray(w, BF16), jnp.asarray(np.stack(masks), F32)))
    return out


def _lower_bound(logits, layer):
    rows = [logits[j:j + 1, :] for j in range(DEPTH)]
    mx = functools.reduce(jnp.maximum, rows)
    es = [jnp.exp(r - mx) for r in rows]
    tot = functools.reduce(jnp.add, es)
    acc = jnp.zeros_like(mx)
    for j in range(1, layer + 1):
        acc = acc + es[j] / tot
    return acc


def _boundary_rows(b, c, fwd):
    parts = []
    for p in range(GLA_CHUNK // (2 * c)):
        r = p * 2 * c + (c - 1 if fwd else c)
        parts.append(jnp.broadcast_to(b[r:r + 1, :], (2 * c, b.shape[1])))
    return parts[0] if len(parts) == 1 else jnp.concatenate(parts, 0)


def _gla_load(s):
    n = GLA_CHUNK
    s["rows"] = rows = pl.ds(pl.multiple_of(s["c"] * n, n), n)
    half = 0.5 * (1.0 - s["lb"])
    s["v"] = s["v_ref"][rows, :]
    s["q"] = _silu(s["q_ref"][rows, :])
    ht = half * jnp.tanh(0.5 * s["z_ref"][rows, :])
    s["f"] = jnp.maximum((s["lb"] + half) + ht, F_MIN)
    s["k"] = half - ht
    s["g2"] = _split2(jnp.log2(s["f"]))


def _gla_cumsum(s):
    w = s["w_ref"][...]
    s["d"] = _dot(w, s["g2"][0]) + _dot(w, s["g2"][1])


def _gla_decays(s):
    n, fwd = GLA_CHUNK, s["fwd"]
    b = s["d"][0:n]
    b_end = b[n - 1:n, :] if fwd else b[0:1, :]
    s["q_in"] = (s["q"] * jnp.exp2(b)).astype(BF16)
    s["k_out"] = (s["k"] * jnp.exp2(b_end - b)).astype(BF16)
    s["d_all"] = jnp.exp2(b_end)
    es = []
    for c in GLA_LEVELS:
        if c >= 4:
            es.append(jnp.exp2(-jnp.abs(b - _boundary_rows(b, c, fwd))))
        elif c == 2:
            es.append(jnp.exp2(s["d"][n:2 * n]))
        else:
            row = lax.broadcasted_iota(jnp.int32, (n, 1), 0)
            es.append(jnp.where((row & 1) == (1 if fwd else 0), s["f"], 1.0))
    s["es"] = es
    s["ps"] = []


def _gla_level(l, s):
    e = s["es"][l]
    s["ps"].append(_dot_nt((s["q"] * e).astype(BF16), (s["k"] * e).astype(BF16)))


def _gla_intra(s):
    a = None
    for l in range(len(GLA_LEVELS)):
        p = s["m_ref"][l] * s["ps"][l]
        a = p if a is None else a + p
    o_self = jnp.sum(s["q"] * s["k"], -1, keepdims=True) * s["v"]
    s["o"] = o_self + _dot(a.astype(BF16), s["v"].astype(BF16))
    s["upd"] = _dot(s["v"].T.astype(BF16), s["k_out"])


def _gla_state(s):
    st = s["st"][0]
    s["o_ref"][s["rows"], :] = s["o"] + _dot_nt(s["q_in"], st.astype(BF16))
    s["st"][0] = s["d_all"] * st + s["upd"]


_GLA_STAGES = ([_gla_load, _gla_cumsum, _gla_decays]
               + [functools.partial(_gla_level, l) for l in range(len(GLA_LEVELS))]
               + [_gla_intra, _gla_state])


def _gla_streams(groups):
    nst = len(_GLA_STAGES)
    for t in range(nst + GLA_SKEW * (len(groups) - 1)):
        for g, streams in enumerate(groups):
            k = t - GLA_SKEW * g
            if 0 <= k < nst:
                for s in streams:
                    _GLA_STAGES[k](s)


def _gla_kernel(qf_ref, vf_ref, zf_ref, qb_ref, vb_ref, zb_ref, lb_ref, s0f_ref, s0b_ref,
                wf_ref, mf_ref, wb_ref, mb_ref,
                of_ref, ob_ref, sf_ref, sb_ref, stf, stb, *, layer, nchunks):
    step = pl.program_id(1)

    @pl.when(step == 0)
    def _():
        stf[...] = s0f_ref[...]
        stb[...] = s0b_ref[...]

    lbf = _lower_bound(lb_ref[0], layer)
    lbb = _lower_bound(lb_ref[1], layer)

    def body(ci, carry):
        st_f, st_b = [stf[...]], [stb[...]]
        groups = []
        for u in range(GLA_UNROLL):
            c = ci * GLA_UNROLL + u
            groups.append([
                dict(q_ref=qf_ref, v_ref=vf_ref, z_ref=zf_ref, o_ref=of_ref, w_ref=wf_ref,
                     m_ref=mf_ref, lb=lbf, c=c, fwd=True, st=st_f),
                dict(q_ref=qb_ref, v_ref=vb_ref, z_ref=zb_ref, o_ref=ob_ref, w_ref=wb_ref,
                     m_ref=mb_ref, lb=lbb, c=nchunks - 1 - c, fwd=False, st=st_b)])
        _gla_streams(groups)
        stf[...] = st_f[0]
        stb[...] = st_b[0]
        return carry

    lax.fori_loop(0, nchunks // GLA_UNROLL, body, 0)

    @pl.when(step == pl.num_programs(1) - 1)
    def _():
        sf_ref[...] = stf[...]
        sb_ref[...] = stb[...]


def _gla(p, lb_logits, s0f, s0b, consts, *, layer, tb):
    t = p.shape[0]
    nb = t // tb
    hd, h = HEAD_DIM, HGRN_HEADS
    (wf, mf), (wb, mb) = consts

    def col(group, rev):
        if rev:
            return pl.BlockSpec((tb, hd), lambda hh, n: (nb - 1 - n, group * h + hh))
        return pl.BlockSpec((tb, hd), lambda hh, n: (n, group * h + hh))

    state_spec = pl.BlockSpec((None, hd, hd), lambda hh, n: (hh, 0, 0))
    const2 = lambda a: pl.BlockSpec(a.shape, lambda hh, n: (0,) * a.ndim)
    return pl.pallas_call(
        functools.partial(_gla_kernel, layer=layer, nchunks=tb // GLA_CHUNK),
        grid=(h, nb),
        in_specs=[
            col(0, False), col(1, False), col(2, False),
            col(0, True), col(1, True), col(3, True),
            pl.BlockSpec((2, DEPTH, hd), lambda hh, n: (0, 0, hh)),
            state_spec, state_spec,
            const2(wf), const2(mf), const2(wb), const2(mb),
        ],
        out_specs=[
            pl.BlockSpec((tb, hd), lambda hh, n: (n, hh)),
            pl.BlockSpec((tb, hd), lambda hh, n: (nb - 1 - n, hh)),
            state_spec, state_spec,
        ],
        out_shape=[
            jax.ShapeDtypeStruct((t, D_HGRN), F32),
            jax.ShapeDtypeStruct((t, D_HGRN), F32),
            jax.ShapeDtypeStruct((h, hd, hd), F32),
            jax.ShapeDtypeStruct((h, hd, hd), F32),
        ],
        scratch_shapes=[pltpu.VMEM((hd, hd), F32), pltpu.VMEM((hd, hd), F32)],
        compiler_params=_cparams(("arbitrary", "arbitrary")),
        name="gla",
    )(p, p, p, p, p, p, lb_logits, s0f, s0b, wf, mf, wb, mb)


def _pool_constants(period):
    n = MIX_ROWS
    t = np.arange(n)
    tau = t % period
    base = t - tau
    bands, cnts = [], []
    for w in POOL_WINDOWS:
        lo = np.maximum(tau - w // 2, 0)
        hi = np.minimum(tau + w // 2, period)
        s = np.arange(n)[None, :]
        bands.append(((s >= (base + lo)[:, None]) & (s < (base + hi)[:, None])).astype(np.float32))
        cnts.append(np.broadcast_to((hi - lo).astype(np.float32)[:, None], (n, POOL_GROUP)))
    return jnp.asarray(np.stack(bands), BF16), jnp.asarray(np.stack(cnts), F32)


def _row_edge_masks(nrows, period):
    row = lax.broadcasted_iota(jnp.int32, (nrows, 1), 0)
    tau = row & (period - 1)
    return tau != 0, tau != period - 1


def _dwconv3(x, w, has_left, has_right):
    nrows = x.shape[0]
    left = jnp.where(has_left, pltpu.roll(x, 1, 0), 0.0)
    right = jnp.where(has_right, pltpu.roll(x, nrows - 1, 0), 0.0)
    return w[0:1, :] * left + w[1:2, :] * x + w[2:3, :] * right


def _mix_out_kernel(x_ref, g_ref, vb_ref, ch_ref, of_ref, ob_ref, nw_ref, pw_ref, ps_ref, cw_ref,
                    band_ref, cnt_ref, wout_ref, g1_ref, lng_ref, lnb_ref, o_ref, cat_ref,
                    *, mod_row, period, tm):
    has_left, has_right = _row_edge_masks(MIX_ROWS, period)
    g1 = g1_ref[mod_row:mod_row + 1, :]
    for sb in range(tm // MIX_ROWS):
        rows = slice(sb * MIX_ROWS, (sb + 1) * MIX_ROWS)
        for h in range(HGRN_HEADS):
            cols = slice(h * HEAD_DIM, (h + 1) * HEAD_DIM)
            o = of_ref[rows, cols] + ob_ref[rows, cols]
            o = o * lax.rsqrt(jnp.mean(o * o, -1, keepdims=True) + EPS) * nw_ref[:, cols]
            cat_ref[rows, cols] = (o * _silu(g_ref[rows, cols])).astype(BF16)
        for gi in range(len(POOL_WINDOWS)):
            cols = slice(gi * POOL_GROUP, (gi + 1) * POOL_GROUP)
            v = vb_ref[rows, cols]
            v_hi, v_mid = _split2(v)
            band = band_ref[gi]
            dlt = (_dot(band, v_hi) + _dot(band, v_mid)) / cnt_ref[gi] - v
            y = _dot(dlt.astype(BF16), pw_ref[gi]) * ps_ref[:, cols]
            cat_ref[rows, D_HGRN + gi * POOL_GROUP:D_HGRN + (gi + 1) * POOL_GROUP] = y.astype(BF16)
        bg = vb_ref[rows, D_POOL:]
        ch = ch_ref[rows, :D_CONV] * ch_ref[rows, D_CONV:]
        cat_ref[rows, D_HGRN + D_POOL:] = (bg * _dwconv3(ch, cw_ref, has_left, has_right)).astype(BF16)

        y = _dot(cat_ref[rows, :], wout_ref[...])
        r = ALPHA * x_ref[rows, :] + g1 * y
        o_ref[rows, :] = _layer_norm(r) * lng_ref[...] + lnb_ref[...]


def _mix_out(x, p, o_f, o_b, mods, norm_w, pool_w, pool_scale, conv_w, pool_consts, w_out,
             ln_g, ln_b, *, layer, mod_row, period, tm):
    m, d = x.shape
    band, cnt = pool_consts
    const = lambda a: pl.BlockSpec(a.shape, lambda i: (0,) * a.ndim)
    row = lambda width, cb: pl.BlockSpec((tm, width), lambda i: (i, cb))
    vec = lambda a: a.reshape(1, -1)
    args = [
        (x, row(d, 0)),
        (p, row(D_HGRN, 4)),
        (p, row(D_POOL + D_CONV, 5)),
        (p, row(2 * D_CONV, 6)),
        (o_f, row(D_HGRN, 0)),
        (o_b, row(D_HGRN, 0)),
        (vec(norm_w), None), (pool_w, None), (vec(pool_scale), None), (conv_w, None),
        (band, None), (cnt, None),
        (w_out, pl.BlockSpec((None, d, d), lambda i: (layer, 0, 0))),
        (mods, pl.BlockSpec((ADA_ROWS, d), lambda i: (0, 2))),
        (vec(ln_g), None), (vec(ln_b), None),
    ]
    return pl.pallas_call(
        functools.partial(_mix_out_kernel, mod_row=mod_row, period=period, tm=tm),
        grid=(m // tm,),
        in_specs=[s if s is not None else const(a) for a, s in args],
        out_specs=pl.BlockSpec((tm, d), lambda i: (i, 0)),
        out_shape=jax.ShapeDtypeStruct((m, d), F32),
        scratch_shapes=[pltpu.VMEM((tm, d), BF16)],
        compiler_params=_cparams(("arbitrary",)),
        name="mix_out",
    )(*[a for a, _ in args])


def _ffn_kernel(x_ref, sh_ref, sc_ref, g2_ref, wu_ref, wg_ref, cw_ref, cb_ref, wd_ref,
                lng_ref, lnb_ref, o_ref, h_ref, *, mod_row, period, tm):
    j = pl.program_id(1)

    @pl.when(j == 0)
    def _():
        sh = sh_ref[mod_row:mod_row + 1, :]
        sc = sc_ref[mod_row:mod_row + 1, :]
        h_ref[...] = (_layer_norm(x_ref[...]) * (1.0 + sc) + sh).astype(BF16)
        o_ref[...] = jnp.zeros_like(o_ref)

    has_left, has_right = _row_edge_masks(tm, period)
    h = h_ref[...]
    width = wu_ref.shape[1] // FFN_SPLIT
    subs = [slice(a * width, (a + 1) * width) for a in range(FFN_SPLIT)]
    ups = [(_dot(h, wu_ref[:, cols]), _dot(h, wg_ref[:, cols])) for cols in subs]
    part = None
    for cols, (u, g) in zip(subs, ups):
        act = u * _silu(_dwconv3(g, cw_ref[:, cols], has_left, has_right) + cb_ref[:, cols])
        p = _dot(act.astype(BF16), wd_ref[cols, :])
        part = p if part is None else part + p
    o_ref[...] += part

    @pl.when(j == pl.num_programs(1) - 1)
    def _():
        r = ALPHA * x_ref[...] + g2_ref[mod_row:mod_row + 1, :] * o_ref[...]
        o_ref[...] = _layer_norm(r) * lng_ref[...] + lnb_ref[...]


def _ffn(x, mods, w_up, conv_w, conv_b, w_down, ln_g, ln_b, *, layer, mod_row, period, tm, tf):
    m, d = x.shape
    nf = D_FF // tf
    vec = lambda a: a.reshape(1, -1)
    mod = lambda cb: pl.BlockSpec((ADA_ROWS, d), lambda i, j: (0, cb))
    const = lambda a: pl.BlockSpec(a.shape, lambda i, j: (0,) * a.ndim)
    return pl.pallas_call(
        functools.partial(_ffn_kernel, mod_row=mod_row, period=period, tm=tm),
        grid=(m // tm, nf),
        in_specs=[
            pl.BlockSpec((tm, d), lambda i, j: (i, 0)),
            mod(3), mod(4), mod(5),
            pl.BlockSpec((None, d, tf), lambda i, j: (layer, 0, j)),
            pl.BlockSpec((None, d, tf), lambda i, j: (layer, 0, nf + j)),
            pl.BlockSpec((3, tf), lambda i, j: (0, j)),
            pl.BlockSpec((1, tf), lambda i, j: (0, j)),
            pl.BlockSpec((None, tf, d), lambda i, j: (layer, j, 0)),
            const(vec(ln_g)), const(vec(ln_b)),
        ],
        out_specs=pl.BlockSpec((tm, d), lambda i, j: (i, 0)),
        out_shape=jax.ShapeDtypeStruct((m, d), F32),
        scratch_shapes=[pltpu.VMEM((tm, d), BF16)],
        compiler_params=_cparams(("arbitrary", "arbitrary")),
        name="ffn",
    )(x, mods, mods, mods, w_up, w_up, conv_w, vec(conv_b), w_down, vec(ln_g), vec(ln_b))


def kernel(x, c, ctx, c_ctx, w_ada, b_ada, w_in, lb_logits, hgrn_norm_w, pool_w, pool_scale, conv_w,
           w_out, ln1_g, ln1_b, w_up, ffn_conv_w, ffn_conv_b, w_down, ln2_g, ln2_b):
    assert x.shape[0] == 1 and ctx.shape[0] == 1, "batch size 1 only"
    xs, cs = x[0], ctx[0]
    seq, ctx_len = xs.shape[0], cs.shape[0]

    cc = jnp.concatenate([c, c_ctx[None, :], jnp.zeros((ADA_ROWS - 2, D_MODEL), F32)], 0)
    mods_all = _ada(cc, w_ada, b_ada)

    gla_consts = _gla_constants()
    pool_x = _pool_constants(GRID_W)
    pool_c = _pool_constants(ctx_len)
    zero_state = jnp.zeros((HGRN_HEADS, HEAD_DIM, HEAD_DIM), F32)
    w_in_b, w_out_b = w_in.astype(BF16), w_out.astype(BF16)
    w_up_b, w_down_b, pool_w_b = w_up.astype(BF16), w_down.astype(BF16), pool_w.astype(BF16)

    for l in range(DEPTH):
        ctx_live = l < DEPTH - 1
        mods = mods_all[l]
        pc = _in_proj(cs, mods, w_in_b, layer=l, mod_row=1, tm=ctx_len, tn=1024)
        px = _in_proj(xs, mods, w_in_b, layer=l, mod_row=0, tm=1024, tn=1024)
        ocf, ocb, s_f, s_b = _gla(pc, lb_logits, zero_state, zero_state, gla_consts, layer=l, tb=ctx_len)
        oxf, oxb, _, _ = _gla(px, lb_logits, s_f, s_b, gla_consts, layer=l, tb=1024)
        mix = functools.partial(_mix_out, mods=mods, norm_w=hgrn_norm_w[l], pool_w=pool_w_b[l],
                                pool_scale=pool_scale[l], conv_w=conv_w[l], w_out=w_out_b,
                                ln_g=ln1_g[l], ln_b=ln1_b[l], layer=l, tm=MIX_ROWS)
        ffn = functools.partial(_ffn, mods=mods, w_up=w_up_b, conv_w=ffn_conv_w[l],
                                conv_b=ffn_conv_b[l], w_down=w_down_b, ln_g=ln2_g[l], ln_b=ln2_b[l],
                                layer=l, tf=512)
        xs = mix(xs, px, oxf, oxb, pool_consts=pool_x, mod_row=0, period=GRID_W)
        if ctx_live:
            cs = mix(cs, pc, ocf, ocb, pool_consts=pool_c, mod_row=1, period=ctx_len)
        xs = ffn(xs, mod_row=0, period=GRID_W, tm=512)
        if ctx_live:
            cs = ffn(cs, mod_row=1, period=ctx_len, tm=ctx_len)
    return xs[None]
```

```python
import functools

import jax
import jax.numpy as jnp
import numpy as np
from jax import lax
from jax.experimental import pallas as pl
from jax.experimental.pallas import tpu as pltpu

D_MODEL = 2048
DEPTH = 2
GRID_W = 64
CTX_LEN = 256
D_HGRN = 1024
HGRN_HEADS = 8
HEAD_DIM = D_HGRN // HGRN_HEADS
D_POOL = 512
POOL_WINDOWS = (2, 4, 8, 16)
POOL_GROUP = D_POOL // len(POOL_WINDOWS)
D_CONV = 512
D_FF = 5632
D_IN = 5 * D_HGRN + D_POOL + 3 * D_CONV
ALPHA = (2 * DEPTH) ** 0.25
EPS = 1e-6
F_MIN = 1e-30
F32 = jnp.float32
BF16 = jnp.bfloat16

GLA_CHUNK = 64
GLA_LEVELS = (32, 16, 8, 4, 2, 1)
GLA_UNROLL = 8
GLA_SKEW = 0
MIX_ROWS = 256
FFN_SPLIT = 2
ADA_ROWS = 8
VMEM_LIMIT = 56 * 1024 * 1024


def _cparams(sem):
    return pltpu.CompilerParams(dimension_semantics=sem, vmem_limit_bytes=VMEM_LIMIT)


def _silu(z):
    h = 0.5 * z
    return h + h * jnp.tanh(h)


def _layer_norm(x):
    mu = jnp.mean(x, -1, keepdims=True)
    xc = x - mu
    var = jnp.mean(xc * xc, -1, keepdims=True)
    return xc * lax.rsqrt(var + EPS)


def _dot_nt(a, b):
    return lax.dot_general(a, b, (((1,), (1,)), ((), ())), preferred_element_type=F32)


def _dot(a, b):
    return jnp.dot(a, b, preferred_element_type=F32)


def _neg_abs(a):
    bits = lax.bitcast_convert_type(a, jnp.uint32) | jnp.uint32(0x80000000)
    return lax.bitcast_convert_type(bits, F32)


def _split2(a):
    hi = a.astype(BF16)
    mid = (a - hi.astype(F32)).astype(BF16)
    return hi, mid


def _ada_kernel(cc_ref, w_ref, b_ref, o_ref):
    a = _silu(cc_ref[...])
    o_ref[...] = jnp.dot(a, w_ref[...], preferred_element_type=F32,
                         precision=lax.Precision.HIGHEST) + b_ref[...]


def _ada(cc, w_ada, b_ada):
    depth, d, n = w_ada.shape
    tn = 1024
    return pl.pallas_call(
        _ada_kernel,
        grid=(depth, n // tn),
        in_specs=[
            pl.BlockSpec((ADA_ROWS, d), lambda l, j: (0, 0)),
            pl.BlockSpec((None, d, tn), lambda l, j: (l, 0, j)),
            pl.BlockSpec((None, 1, tn), lambda l, j: (l, 0, j)),
        ],
        out_specs=pl.BlockSpec((None, ADA_ROWS, tn), lambda l, j: (l, 0, j)),
        out_shape=jax.ShapeDtypeStruct((depth, ADA_ROWS, n), F32),
        compiler_params=_cparams(("arbitrary", "arbitrary")),
        name="ada",
    )(cc, w_ada, b_ada.reshape(depth, 1, n))


def _in_proj_kernel(x_ref, sh_ref, sc_ref, w_ref, o_ref, h_ref, *, mod_row):
    @pl.when(pl.program_id(1) == 0)
    def _():
        sh = sh_ref[mod_row:mod_row + 1, :]
        sc = sc_ref[mod_row:mod_row + 1, :]
        h_ref[...] = (_layer_norm(x_ref[...]) * (1.0 + sc) + sh).astype(BF16)

    o_ref[...] = _dot(h_ref[...], w_ref[...])


def _in_proj(x, mods, w, *, layer, mod_row, tm, tn):
    m, d = x.shape
    n = w.shape[2]
    return pl.pallas_call(
        functools.partial(_in_proj_kernel, mod_row=mod_row),
        grid=(m // tm, n // tn),
        in_specs=[
            pl.BlockSpec((tm, d), lambda i, j: (i, 0)),
            pl.BlockSpec((ADA_ROWS, d), lambda i, j: (0, 0)),
            pl.BlockSpec((ADA_ROWS, d), lambda i, j: (0, 1)),
            pl.BlockSpec((None, d, tn), lambda i, j: (layer, 0, j)),
        ],
        out_specs=pl.BlockSpec((tm, tn), lambda i, j: (i, j)),
        out_shape=jax.ShapeDtypeStruct((m, n), F32),
        scratch_shapes=[pltpu.VMEM((tm, d), BF16)],
        compiler_params=_cparams(("arbitrary", "arbitrary")),
        name="in_proj",
    )(x, mods, mods, w)


def _gla_constants():
    n = GLA_CHUNK
    t = np.arange(n)[:, None]
    r = np.arange(n)[None, :]
    out = []
    for fwd in (True, False):
        masks = []
        for c in GLA_LEVELS:
            mid = (t // (2 * c)) * (2 * c) + c
            mid_s = (r // (2 * c)) * (2 * c) + c
            same = (t // (2 * c)) == (r // (2 * c))
            masks.append(same & ((t >= mid) & (r < mid_s) if fwd else (t < mid) & (r >= mid_s)))
        mid = (t // 4) * 4 + 2
        if fwd:
            cum = r <= t
            lvl2 = ((t >= mid) & (r >= mid) & (r <= t)) | ((t < mid) & (r > t) & (r <= mid - 1))
        else:
            cum = r >= t
            lvl2 = ((t >= mid) & (r >= mid) & (r <= t - 1)) | ((t < mid) & (r >= t) & (r <= mid - 1))
        w = np.concatenate([cum, lvl2], 0).astype(np.float32)
        out.append((jnp.asarray(w, BF16), jnp.asarray(np.stack(masks), F32)))
    return out


def _lower_bound(logits, layer):
    rows = [logits[j:j + 1, :] for j in range(DEPTH)]
    mx = functools.reduce(jnp.maximum, rows)
    es = [jnp.exp(r - mx) for r in rows]
    tot = functools.reduce(jnp.add, es)
    acc = jnp.zeros_like(mx)
    for j in range(1, layer + 1):
        acc = acc + es[j] / tot
    return acc


def _boundary_rows(b, c, fwd):
    parts = []
    for p in range(GLA_CHUNK // (2 * c)):
        r = p * 2 * c + (c - 1 if fwd else c)
        parts.append(jnp.broadcast_to(b[r:r + 1, :], (2 * c, b.shape[1])))
    return parts[0] if len(parts) == 1 else jnp.concatenate(parts, 0)


def _gla_load(s):
    n = GLA_CHUNK
    s["rows"] = rows = pl.ds(pl.multiple_of(s["c"] * n, n), n)
    half = 0.5 * (1.0 - s["lb"])
    s["v"] = s["v_ref"][rows, :]
    s["q"] = _silu(s["q_ref"][rows, :])
    ht = half * jnp.tanh(0.5 * s["z_ref"][rows, :])
    s["f"] = jnp.maximum((s["lb"] + half) + ht, F_MIN)
    s["k"] = half - ht
    s["g2"] = _split2(jnp.log2(s["f"]))


def _gla_cumsum(s):
    w = s["w_ref"][...]
    s["d"] = _dot(w, s["g2"][0]) + _dot(w, s["g2"][1])


def _gla_decays(s):
    n, fwd = GLA_CHUNK, s["fwd"]
    b = s["d"][0:n]
    b_end = b[n - 1:n, :] if fwd else b[0:1, :]
    s["q_in"] = (s["q"] * jnp.exp2(b)).astype(BF16)
    s["k_out"] = (s["k"] * jnp.exp2(b_end - b)).astype(BF16)
    s["d_all"] = jnp.exp2(b_end)
    es = []
    for c in GLA_LEVELS:
        if c >= 4:
            es.append(jnp.exp2(_neg_abs(b - _boundary_rows(b, c, fwd))))
        elif c == 2:
            es.append(jnp.exp2(s["d"][n:2 * n]))
        else:
            row = lax.broadcasted_iota(jnp.int32, (n, 1), 0)
            es.append(jnp.where((row & 1) == (1 if fwd else 0), s["f"], 1.0))
    s["es"] = [e.astype(BF16) for e in es]
    s["qk16"] = s["q"].astype(BF16), s["k"].astype(BF16)
    s["ps"] = []


def _gla_level(l, s):
    e = s["es"][l]
    q16, k16 = s["qk16"]
    s["ps"].append(_dot_nt(q16 * e, k16 * e))


def _gla_intra(s):
    a = None
    for l in range(len(GLA_LEVELS)):
        p = s["m_ref"][l] * s["ps"][l]
        a = p if a is None else a + p
    o_self = jnp.sum(s["q"] * s["k"], -1, keepdims=True) * s["v"]
    s["o"] = o_self + _dot(a.astype(BF16), s["v"].astype(BF16))
    s["upd"] = _dot(s["v"].T.astype(BF16), s["k_out"])


def _gla_state(s):
    st = s["st"][0]
    s["o_ref"][s["rows"], :] = s["o"] + _dot_nt(s["q_in"], st.astype(BF16))
    s["st"][0] = s["d_all"] * st + s["upd"]


_GLA_STAGES = ([_gla_load, _gla_cumsum, _gla_decays]
               + [functools.partial(_gla_level, l) for l in range(len(GLA_LEVELS))]
               + [_gla_intra, _gla_state])


def _gla_streams(groups):
    nst = len(_GLA_STAGES)
    for t in range(nst + GLA_SKEW * (len(groups) - 1)):
        for g, streams in enumerate(groups):
            k = t - GLA_SKEW * g
            if 0 <= k < nst:
                for s in streams:
                    _GLA_STAGES[k](s)


def _gla_kernel(qf_ref, vf_ref, zf_ref, qb_ref, vb_ref, zb_ref, lb_ref, s0f_ref, s0b_ref,
                wf_ref, mf_ref, wb_ref, mb_ref,
                of_ref, ob_ref, sf_ref, sb_ref, stf, stb, *, layer, nchunks):
    step = pl.program_id(1)

    @pl.when(step == 0)
    def _():
        stf[...] = s0f_ref[...]
        stb[...] = s0b_ref[...]

    lbf = _lower_bound(lb_ref[0], layer)
    lbb = _lower_bound(lb_ref[1], layer)

    unroll = min(GLA_UNROLL, nchunks)
    assert nchunks % unroll == 0

    def body(ci, carry):
        st_f, st_b = [stf[...]], [stb[...]]
        groups = []
        for u in range(unroll):
            c = ci * unroll + u
            groups.append([
                dict(q_ref=qf_ref, v_ref=vf_ref, z_ref=zf_ref, o_ref=of_ref, w_ref=wf_ref,
                     m_ref=mf_ref, lb=lbf, c=c, fwd=True, st=st_f),
                dict(q_ref=qb_ref, v_ref=vb_ref, z_ref=zb_ref, o_ref=ob_ref, w_ref=wb_ref,
                     m_ref=mb_ref, lb=lbb, c=nchunks - 1 - c, fwd=False, st=st_b)])
        _gla_streams(groups)
        stf[...] = st_f[0]
        stb[...] = st_b[0]
        return carry

    lax.fori_loop(0, nchunks // unroll, body, 0)

    @pl.when(step == pl.num_programs(1) - 1)
    def _():
        sf_ref[...] = stf[...]
        sb_ref[...] = stb[...]


def _gla(p, lb_logits, s0f, s0b, consts, *, layer, tb):
    t = p.shape[0]
    nb = t // tb
    hd, h = HEAD_DIM, HGRN_HEADS
    (wf, mf), (wb, mb) = consts

    def col(group, rev):
        if rev:
            return pl.BlockSpec((tb, hd), lambda hh, n: (nb - 1 - n, group * h + hh))
        return pl.BlockSpec((tb, hd), lambda hh, n: (n, group * h + hh))

    state_spec = pl.BlockSpec((None, hd, hd), lambda hh, n: (hh, 0, 0))
    const2 = lambda a: pl.BlockSpec(a.shape, lambda hh, n: (0,) * a.ndim)
    return pl.pallas_call(
        functools.partial(_gla_kernel, layer=layer, nchunks=tb // GLA_CHUNK),
        grid=(h, nb),
        in_specs=[
            col(0, False), col(1, False), col(2, False),
            col(0, True), col(1, True), col(3, True),
            pl.BlockSpec((2, DEPTH, hd), lambda hh, n: (0, 0, hh)),
            state_spec, state_spec,
            const2(wf), const2(mf), const2(wb), const2(mb),
        ],
        out_specs=[
            pl.BlockSpec((tb, hd), lambda hh, n: (n, hh)),
            pl.BlockSpec((tb, hd), lambda hh, n: (nb - 1 - n, hh)),
            state_spec, state_spec,
        ],
        out_shape=[
            jax.ShapeDtypeStruct((t, D_HGRN), F32),
            jax.ShapeDtypeStruct((t, D_HGRN), F32),
            jax.ShapeDtypeStruct((h, hd, hd), F32),
            jax.ShapeDtypeStruct((h, hd, hd), F32),
        ],
        scratch_shapes=[pltpu.VMEM((hd, hd), F32), pltpu.VMEM((hd, hd), F32)],
        compiler_params=_cparams(("arbitrary", "arbitrary")),
        name="gla",
    )(p, p, p, p, p, p, lb_logits, s0f, s0b, wf, mf, wb, mb)


def _pool_constants(period):
    n = MIX_ROWS
    t = np.arange(n)
    tau = t % period
    base = t - tau
    bands, cnts = [], []
    for w in POOL_WINDOWS:
        lo = np.maximum(tau - w // 2, 0)
        hi = np.minimum(tau + w // 2, period)
        s = np.arange(n)[None, :]
        bands.append(((s >= (base + lo)[:, None]) & (s < (base + hi)[:, None])).astype(np.float32))
        cnts.append(np.broadcast_to((hi - lo).astype(np.float32)[:, None], (n, POOL_GROUP)))
    return jnp.asarray(np.stack(bands), BF16), jnp.asarray(np.stack(cnts), F32)


def _row_edge_masks(nrows, period):
    row = lax.broadcasted_iota(jnp.int32, (nrows, 1), 0)
    tau = row & (period - 1)
    return tau != 0, tau != period - 1


def _dwconv3(x, w, has_left, has_right):
    nrows = x.shape[0]
    left = jnp.where(has_left, pltpu.roll(x, 1, 0), 0.0)
    right = jnp.where(has_right, pltpu.roll(x, nrows - 1, 0), 0.0)
    return w[0:1, :] * left + w[1:2, :] * x + w[2:3, :] * right


def _mix_out_kernel(x_ref, g_ref, vb_ref, ch_ref, of_ref, ob_ref, nw_ref, pw_ref, ps_ref, cw_ref,
                    band_ref, cnt_ref, wout_ref, g1_ref, lng_ref, lnb_ref, o_ref, cat_ref,
                    *, mod_row, period, tm):
    has_left, has_right = _row_edge_masks(MIX_ROWS, period)
    g1 = g1_ref[mod_row:mod_row + 1, :]
    blocks = [slice(sb * MIX_ROWS, (sb + 1) * MIX_ROWS) for sb in range(tm // MIX_ROWS)]
    for rows in blocks:
        for h in range(HGRN_HEADS):
            cols = slice(h * HEAD_DIM, (h + 1) * HEAD_DIM)
            o = of_ref[rows, cols] + ob_ref[rows, cols]
            o = o * lax.rsqrt(jnp.mean(o * o, -1, keepdims=True) + EPS) * nw_ref[:, cols]
            cat_ref[rows, cols] = (o * _silu(g_ref[rows, cols])).astype(BF16)
        for gi in range(len(POOL_WINDOWS)):
            cols = slice(gi * POOL_GROUP, (gi + 1) * POOL_GROUP)
            v = vb_ref[rows, cols]
            v_hi, v_mid = _split2(v)
            band = band_ref[gi]
            dlt = (_dot(band, v_hi) + _dot(band, v_mid)) / cnt_ref[gi] - v
            y = _dot(dlt.astype(BF16), pw_ref[gi]) * ps_ref[:, cols]
            cat_ref[rows, D_HGRN + gi * POOL_GROUP:D_HGRN + (gi + 1) * POOL_GROUP] = y.astype(BF16)
        bg = vb_ref[rows, D_POOL:]
        ch = ch_ref[rows, :D_CONV] * ch_ref[rows, D_CONV:]
        cat_ref[rows, D_HGRN + D_POOL:] = (bg * _dwconv3(ch, cw_ref[...], has_left, has_right)).astype(BF16)
    ys = [_dot(cat_ref[rows, :], wout_ref[...]) for rows in blocks]
    for rows, y in zip(blocks, ys):
        r = ALPHA * x_ref[rows, :] + g1 * y
        o_ref[rows, :] = _layer_norm(r) * lng_ref[...] + lnb_ref[...]


def _mix_out(x, p, o_f, o_b, mods, norm_w, pool_w, pool_scale, conv_w, pool_consts, w_out,
             ln_g, ln_b, *, layer, mod_row, period, tm):
    m, d = x.shape
    band, cnt = pool_consts
    const = lambda a: pl.BlockSpec(a.shape, lambda i: (0,) * a.ndim)
    row = lambda width, cb: pl.BlockSpec((tm, width), lambda i: (i, cb))
    vec = lambda a: a.reshape(1, -1)
    args = [
        (x, row(d, 0)),
        (p, row(D_HGRN, 4)),
        (p, row(D_POOL + D_CONV, 5)),
        (p, row(2 * D_CONV, 6)),
        (o_f, row(D_HGRN, 0)),
        (o_b, row(D_HGRN, 0)),
        (vec(norm_w), None), (pool_w, None), (vec(pool_scale), None), (conv_w, None),
        (band, None), (cnt, None),
        (w_out, pl.BlockSpec((None, d, d), lambda i: (layer, 0, 0))),
        (mods, pl.BlockSpec((ADA_ROWS, d), lambda i: (0, 2))),
        (vec(ln_g), None), (vec(ln_b), None),
    ]
    return pl.pallas_call(
        functools.partial(_mix_out_kernel, mod_row=mod_row, period=period, tm=tm),
        grid=(m // tm,),
        in_specs=[s if s is not None else const(a) for a, s in args],
        out_specs=pl.BlockSpec((tm, d), lambda i: (i, 0)),
        out_shape=jax.ShapeDtypeStruct((m, d), F32),
        scratch_shapes=[pltpu.VMEM((tm, d), BF16)],
        compiler_params=_cparams(("arbitrary",)),
        name="mix_out",
    )(*[a for a, _ in args])


def _ffn_kernel(x_ref, sh_ref, sc_ref, g2_ref, wu_ref, wg_ref, cw_ref, cb_ref, wd_ref,
                lng_ref, lnb_ref, o_ref, h_ref, *, mod_row, period, tm):
    j = pl.program_id(1)

    @pl.when(j == 0)
    def _():
        sh = sh_ref[mod_row:mod_row + 1, :]
        sc = sc_ref[mod_row:mod_row + 1, :]
        h_ref[...] = (_layer_norm(x_ref[...]) * (1.0 + sc) + sh).astype(BF16)
        o_ref[...] = jnp.zeros_like(o_ref)

    has_left, has_right = _row_edge_masks(tm, period)
    h = h_ref[...]
    width = wu_ref.shape[1] // FFN_SPLIT
    subs = [slice(a * width, (a + 1) * width) for a in range(FFN_SPLIT)]
    ups = [(_dot(h, wu_ref[:, cols]), _dot(h, wg_ref[:, cols])) for cols in subs]
    for cols, (u, g) in zip(subs, ups):
        act = u * _silu(_dwconv3(g, cw_ref[:, cols], has_left, has_right) + cb_ref[:, cols])
        o_ref[...] += _dot(act.astype(BF16), wd_ref[cols, :])

    @pl.when(j == pl.num_programs(1) - 1)
    def _():
        r = ALPHA * x_ref[...] + g2_ref[mod_row:mod_row + 1, :] * o_ref[...]
        o_ref[...] = _layer_norm(r) * lng_ref[...] + lnb_ref[...]


def _ffn(x, mods, w_up, conv_w, conv_b, w_down, ln_g, ln_b, *, layer, mod_row, period, tm, tf):
    m, d = x.shape
    nf = D_FF // tf
    vec = lambda a: a.reshape(1, -1)
    mod = lambda cb: pl.BlockSpec((ADA_ROWS, d), lambda i, j: (0, cb))
    const = lambda a: pl.BlockSpec(a.shape, lambda i, j: (0,) * a.ndim)
    return pl.pallas_call(
        functools.partial(_ffn_kernel, mod_row=mod_row, period=period, tm=tm),
        grid=(m // tm, nf),
        in_specs=[
            pl.BlockSpec((tm, d), lambda i, j: (i, 0)),
            mod(3), mod(4), mod(5),
            pl.BlockSpec((None, d, tf), lambda i, j: (layer, 0, j)),
            pl.BlockSpec((None, d, tf), lambda i, j: (layer, 0, nf + j)),
            pl.BlockSpec((3, tf), lambda i, j: (0, j)),
            pl.BlockSpec((1, tf), lambda i, j: (0, j)),
            pl.BlockSpec((None, tf, d), lambda i, j: (layer, j, 0)),
            const(vec(ln_g)), const(vec(ln_b)),
        ],
        out_specs=pl.BlockSpec((tm, d), lambda i, j: (i, 0)),
        out_shape=jax.ShapeDtypeStruct((m, d), F32),
        scratch_shapes=[pltpu.VMEM((tm, d), BF16)],
        compiler_params=_cparams(("arbitrary", "arbitrary")),
        name="ffn",
    )(x, mods, mods, mods, w_up, w_up, conv_w, vec(conv_b), w_down, vec(ln_g), vec(ln_b))


def kernel(x, c, ctx, c_ctx, w_ada, b_ada, w_in, lb_logits, hgrn_norm_w, pool_w, pool_scale, conv_w,
           w_out, ln1_g, ln1_b, w_up, ffn_conv_w, ffn_conv_b, w_down, ln2_g, ln2_b):
    assert x.shape[0] == 1 and ctx.shape[0] == 1, "batch size 1 only"
    xs, cs = x[0], ctx[0]
    seq, ctx_len = xs.shape[0], cs.shape[0]

    cc = jnp.concatenate([c, c_ctx[None, :], jnp.zeros((ADA_ROWS - 2, D_MODEL), F32)], 0)
    mods_all = _ada(cc, w_ada, b_ada)

    gla_consts = _gla_constants()
    pool_x = _pool_constants(GRID_W)
    pool_c = _pool_constants(ctx_len)
    zero_state = jnp.zeros((HGRN_HEADS, HEAD_DIM, HEAD_DIM), F32)
    w_in_b, w_out_b = w_in.astype(BF16), w_out.astype(BF16)
    w_up_b, w_down_b, pool_w_b = w_up.astype(BF16), w_down.astype(BF16), pool_w.astype(BF16)

    for l in range(DEPTH):
        ctx_live = l < DEPTH - 1
        mods = mods_all[l]
        pc = _in_proj(cs, mods, w_in_b, layer=l, mod_row=1, tm=ctx_len, tn=1024)
        px = _in_proj(xs, mods, w_in_b, layer=l, mod_row=0, tm=1024, tn=1024)
        ocf, ocb, s_f, s_b = _gla(pc, lb_logits, zero_state, zero_state, gla_consts, layer=l, tb=ctx_len)
        oxf, oxb, _, _ = _gla(px, lb_logits, s_f, s_b, gla_consts, layer=l, tb=1024)
        mix = functools.partial(_mix_out, mods=mods, norm_w=hgrn_norm_w[l], pool_w=pool_w_b[l],
                                pool_scale=pool_scale[l], conv_w=conv_w[l], w_out=w_out_b,
                                ln_g=ln1_g[l], ln_b=ln1_b[l], layer=l)
        ffn = functools.partial(_ffn, mods=mods, w_up=w_up_b, conv_w=ffn_conv_w[l],
                                conv_b=ffn_conv_b[l], w_down=w_down_b, ln_g=ln2_g[l], ln_b=ln2_b[l],
                                layer=l, tf=512)
        xs = mix(xs, px, oxf, oxb, pool_consts=pool_x, mod_row=0, period=GRID_W, tm=MIX_ROWS)
        if ctx_live:
            cs = mix(cs, pc, ocf, ocb, pool_consts=pool_c, mod_row=1, period=ctx_len, tm=ctx_len)
        xs = ffn(xs, mod_row=0, period=GRID_W, tm=512)
        if ctx_live:
            cs = ffn(cs, mod_row=1, period=ctx_len, tm=ctx_len)
    return xs[None]
```

```python
import functools

import jax
import jax.numpy as jnp
import numpy as np
from jax import lax
from jax.experimental import pallas as pl
from jax.experimental.pallas import tpu as pltpu

D_MODEL = 2048
DEPTH = 2
GRID_W = 64
CTX_LEN = 256
D_HGRN = 1024
HGRN_HEADS = 8
HEAD_DIM = D_HGRN // HGRN_HEADS
D_POOL = 512
POOL_WINDOWS = (2, 4, 8, 16)
POOL_GROUP = D_POOL // len(POOL_WINDOWS)
D_CONV = 512
D_FF = 5632
D_IN = 5 * D_HGRN + D_POOL + 3 * D_CONV
ALPHA = (2 * DEPTH) ** 0.25
EPS = 1e-6
F_MIN = 1e-30
F32 = jnp.float32
BF16 = jnp.bfloat16

GLA_CHUNK = 64
GLA_LEVELS = (32, 16, 8, 4, 2, 1)
GLA_UNROLL = 16
GLA_SKEW = 0
MIX_ROWS = 256
FFN_SPLIT = 2
ADA_ROWS = 8
VMEM_LIMIT = 56 * 1024 * 1024


def _cparams(sem):
    return pltpu.CompilerParams(dimension_semantics=sem, vmem_limit_bytes=VMEM_LIMIT)


def _silu(z):
    h = 0.5 * z
    return h + h * jnp.tanh(h)


def _layer_norm(x):
    mu = jnp.mean(x, -1, keepdims=True)
    xc = x - mu
    var = jnp.mean(xc * xc, -1, keepdims=True)
    return xc * lax.rsqrt(var + EPS)


def _dot_nt(a, b):
    return lax.dot_general(a, b, (((1,), (1,)), ((), ())), preferred_element_type=F32)


def _dot(a, b):
    return jnp.dot(a, b, preferred_element_type=F32)


def _neg_abs(a):
    bits = lax.bitcast_convert_type(a, jnp.uint32) | jnp.uint32(0x80000000)
    return lax.bitcast_convert_type(bits, F32)


def _split2(a):
    hi = a.astype(BF16)
    mid = (a - hi.astype(F32)).astype(BF16)
    return hi, mid


def _ada_kernel(cc_ref, w_ref, b_ref, o_ref):
    a = _silu(cc_ref[...])
    o_ref[...] = jnp.dot(a, w_ref[...], preferred_element_type=F32,
                         precision=lax.Precision.HIGHEST) + b_ref[...]


def _ada(cc, w_ada, b_ada):
    depth, d, n = w_ada.shape
    tn = 1024
    return pl.pallas_call(
        _ada_kernel,
        grid=(depth, n // tn),
        in_specs=[
            pl.BlockSpec((ADA_ROWS, d), lambda l, j: (0, 0)),
            pl.BlockSpec((None, d, tn), lambda l, j: (l, 0, j)),
            pl.BlockSpec((None, 1, tn), lambda l, j: (l, 0, j)),
        ],
        out_specs=pl.BlockSpec((None, ADA_ROWS, tn), lambda l, j: (l, 0, j)),
        out_shape=jax.ShapeDtypeStruct((depth, ADA_ROWS, n), F32),
        compiler_params=_cparams(("arbitrary", "arbitrary")),
        name="ada",
    )(cc, w_ada, b_ada.reshape(depth, 1, n))


def _in_proj_kernel(x_ref, sh_ref, sc_ref, w_ref, o_ref, h_ref, *, mod_row):
    @pl.when(pl.program_id(1) == 0)
    def _():
        sh = sh_ref[mod_row:mod_row + 1, :]
        sc = sc_ref[mod_row:mod_row + 1, :]
        h_ref[...] = (_layer_norm(x_ref[...]) * (1.0 + sc) + sh).astype(BF16)

    o_ref[...] = _dot(h_ref[...], w_ref[...])


def _in_proj(x, mods, w, *, layer, mod_row, tm, tn):
    m, d = x.shape
    n = w.shape[2]
    return pl.pallas_call(
        functools.partial(_in_proj_kernel, mod_row=mod_row),
        grid=(m // tm, n // tn),
        in_specs=[
            pl.BlockSpec((tm, d), lambda i, j: (i, 0)),
            pl.BlockSpec((ADA_ROWS, d), lambda i, j: (0, 0)),
            pl.BlockSpec((ADA_ROWS, d), lambda i, j: (0, 1)),
            pl.BlockSpec((None, d, tn), lambda i, j: (layer, 0, j)),
        ],
        out_specs=pl.BlockSpec((tm, tn), lambda i, j: (i, j)),
        out_shape=jax.ShapeDtypeStruct((m, n), F32),
        scratch_shapes=[pltpu.VMEM((tm, d), BF16)],
        compiler_params=_cparams(("arbitrary", "arbitrary")),
        name="in_proj",
    )(x, mods, mods, w)


def _gla_constants():
    n = GLA_CHUNK
    t = np.arange(n)[:, None]
    r = np.arange(n)[None, :]
    out = []
    for fwd in (True, False):
        masks = []
        for c in GLA_LEVELS:
            mid = (t // (2 * c)) * (2 * c) + c
            mid_s = (r // (2 * c)) * (2 * c) + c
            same = (t // (2 * c)) == (r // (2 * c))
            masks.append(same & ((t >= mid) & (r < mid_s) if fwd else (t < mid) & (r >= mid_s)))
        mid = (t // 4) * 4 + 2
        if fwd:
            cum = r <= t
            lvl2 = ((t >= mid) & (r >= mid) & (r <= t)) | ((t < mid) & (r > t) & (r <= mid - 1))
        else:
            cum = r >= t
            lvl2 = ((t >= mid) & (r >= mid) & (r <= t - 1)) | ((t < mid) & (r >= t) & (r <= mid - 1))
        w = np.concatenate([cum, lvl2], 0).astype(np.float32)
        out.append((jnp.asarray(w, BF16), jnp.asarray(np.stack(masks), F32)))
    return out


def _lower_bound(logits, layer):
    rows = [logits[j:j + 1, :] for j in range(DEPTH)]
    mx = functools.reduce(jnp.maximum, rows)
    es = [jnp.exp(r - mx) for r in rows]
    tot = functools.reduce(jnp.add, es)
    acc = jnp.zeros_like(mx)
    for j in range(1, layer + 1):
        acc = acc + es[j] / tot
    return acc


def _boundary_rows(b, c, fwd):
    parts = []
    for p in range(GLA_CHUNK // (2 * c)):
        r = p * 2 * c + (c - 1 if fwd else c)
        parts.append(jnp.broadcast_to(b[r:r + 1, :], (2 * c, b.shape[1])))
    return parts[0] if len(parts) == 1 else jnp.concatenate(parts, 0)


def _gla_load(s):
    n = GLA_CHUNK
    s["rows"] = rows = pl.ds(pl.multiple_of(s["c"] * n, n), n)
    half = 0.5 * (1.0 - s["lb"])
    s["v"] = s["v_ref"][rows, :]
    s["q"] = _silu(s["q_ref"][rows, :])
    ht = half * jnp.tanh(0.5 * s["z_ref"][rows, :])
    s["f"] = jnp.maximum((s["lb"] + half) + ht, F_MIN)
    s["k"] = half - ht
    s["g2"] = _split2(jnp.log2(s["f"]))


def _gla_cumsum(s):
    w = s["w_ref"][...]
    s["d"] = _dot(w, s["g2"][0]) + _dot(w, s["g2"][1])


def _gla_decays(s):
    n, fwd = GLA_CHUNK, s["fwd"]
    b = s["d"][0:n]
    b_end = b[n - 1:n, :] if fwd else b[0:1, :]
    s["q_in"] = (s["q"] * jnp.exp2(b)).astype(BF16)
    s["k_out"] = (s["k"] * jnp.exp2(b_end - b)).astype(BF16)
    s["d_all"] = jnp.exp2(b_end)
    es = []
    for c in GLA_LEVELS:
        if c >= 4:
            es.append(jnp.exp2(_neg_abs(b - _boundary_rows(b, c, fwd))))
        elif c == 2:
            es.append(jnp.exp2(s["d"][n:2 * n]))
        else:
            row = lax.broadcasted_iota(jnp.int32, (n, 1), 0)
            es.append(jnp.where((row & 1) == (1 if fwd else 0), s["f"], 1.0))
    s["es"] = [e.astype(BF16) for e in es]
    s["qk16"] = s["q"].astype(BF16), s["k"].astype(BF16)
    s["ps"] = []


def _gla_level(l, s):
    e = s["es"][l]
    q16, k16 = s["qk16"]
    s["ps"].append(_dot_nt(q16 * e, k16 * e))


def _gla_intra(s):
    a = None
    for l in range(len(GLA_LEVELS)):
        p = s["m_ref"][l] * s["ps"][l]
        a = p if a is None else a + p
    o_self = jnp.sum(s["q"] * s["k"], -1, keepdims=True) * s["v"]
    s["o"] = o_self + _dot(a.astype(BF16), s["v"].astype(BF16))
    s["upd"] = _dot(s["v"].T.astype(BF16), s["k_out"])


def _gla_state(s):
    st = s["st"][0]
    s["o_ref"][s["rows"], :] = s["o"] + _dot_nt(s["q_in"], st.astype(BF16))
    s["st"][0] = s["d_all"] * st + s["upd"]


_GLA_STAGES = ([_gla_load, _gla_cumsum, _gla_decays]
               + [functools.partial(_gla_level, l) for l in range(len(GLA_LEVELS))]
               + [_gla_intra, _gla_state])


def _gla_streams(groups):
    nst = len(_GLA_STAGES)
    for t in range(nst + GLA_SKEW * (len(groups) - 1)):
        for g, streams in enumerate(groups):
            k = t - GLA_SKEW * g
            if 0 <= k < nst:
                for s in streams:
                    _GLA_STAGES[k](s)


def _gla_kernel(qf_ref, vf_ref, zf_ref, qb_ref, vb_ref, zb_ref, lb_ref, s0f_ref, s0b_ref,
                wf_ref, mf_ref, wb_ref, mb_ref,
                of_ref, ob_ref, sf_ref, sb_ref, stf, stb, *, layer, nchunks):
    step = pl.program_id(1)

    @pl.when(step == 0)
    def _():
        stf[...] = s0f_ref[...]
        stb[...] = s0b_ref[...]

    lbf = _lower_bound(lb_ref[0], layer)
    lbb = _lower_bound(lb_ref[1], layer)

    unroll = min(GLA_UNROLL, nchunks)
    assert nchunks % unroll == 0

    def body(ci, carry):
        st_f, st_b = [stf[...]], [stb[...]]
        groups = []
        for u in range(unroll):
            c = ci * unroll + u
            groups.append([
                dict(q_ref=qf_ref, v_ref=vf_ref, z_ref=zf_ref, o_ref=of_ref, w_ref=wf_ref,
                     m_ref=mf_ref, lb=lbf, c=c, fwd=True, st=st_f),
                dict(q_ref=qb_ref, v_ref=vb_ref, z_ref=zb_ref, o_ref=ob_ref, w_ref=wb_ref,
                     m_ref=mb_ref, lb=lbb, c=nchunks - 1 - c, fwd=False, st=st_b)])
        _gla_streams(groups)
        stf[...] = st_f[0]
        stb[...] = st_b[0]
        return carry

    lax.fori_loop(0, nchunks // unroll, body, 0)

    @pl.when(step == pl.num_programs(1) - 1)
    def _():
        sf_ref[...] = stf[...]
        sb_ref[...] = stb[...]


def _gla(p, lb_logits, s0f, s0b, consts, *, layer, tb):
    t = p.shape[0]
    nb = t // tb
    hd, h = HEAD_DIM, HGRN_HEADS
    (wf, mf), (wb, mb) = consts

    def col(group, rev):
        if rev:
            return pl.BlockSpec((tb, hd), lambda hh, n: (nb - 1 - n, group * h + hh))
        return pl.BlockSpec((tb, hd), lambda hh, n: (n, group * h + hh))

    state_spec = pl.BlockSpec((None, hd, hd), lambda hh, n: (hh, 0, 0))
    const2 = lambda a: pl.BlockSpec(a.shape, lambda hh, n: (0,) * a.ndim)
    return pl.pallas_call(
        functools.partial(_gla_kernel, layer=layer, nchunks=tb // GLA_CHUNK),
        grid=(h, nb),
        in_specs=[
            col(0, False), col(1, False), col(2, False),
            col(0, True), col(1, True), col(3, True),
            pl.BlockSpec((2, DEPTH, hd), lambda hh, n: (0, 0, hh)),
            state_spec, state_spec,
            const2(wf), const2(mf), const2(wb), const2(mb),
        ],
        out_specs=[
            pl.BlockSpec((tb, hd), lambda hh, n: (n, hh)),
            pl.BlockSpec((tb, hd), lambda hh, n: (nb - 1 - n, hh)),
            state_spec, state_spec,
        ],
        out_shape=[
            jax.ShapeDtypeStruct((t, D_HGRN), F32),
            jax.ShapeDtypeStruct((t, D_HGRN), F32),
            jax.ShapeDtypeStruct((h, hd, hd), F32),
            jax.ShapeDtypeStruct((h, hd, hd), F32),
        ],
        scratch_shapes=[pltpu.VMEM((hd, hd), F32), pltpu.VMEM((hd, hd), F32)],
        compiler_params=_cparams(("arbitrary", "arbitrary")),
        name="gla",
    )(p, p, p, p, p, p, lb_logits, s0f, s0b, wf, mf, wb, mb)


def _pool_constants(period):
    n = MIX_ROWS
    t = np.arange(n)
    tau = t % period
    base = t - tau
    bands, cnts = [], []
    for w in POOL_WINDOWS:
        lo = np.maximum(tau - w // 2, 0)
        hi = np.minimum(tau + w // 2, period)
        s = np.arange(n)[None, :]
        bands.append(((s >= (base + lo)[:, None]) & (s < (base + hi)[:, None])).astype(np.float32))
        cnts.append(np.broadcast_to((hi - lo).astype(np.float32)[:, None], (n, POOL_GROUP)))
    return jnp.asarray(np.stack(bands), BF16), jnp.asarray(np.stack(cnts), F32)


def _row_edge_masks(nrows, period):
    row = lax.broadcasted_iota(jnp.int32, (nrows, 1), 0)
    tau = row & (period - 1)
    return tau != 0, tau != period - 1


def _dwconv3(x, w, has_left, has_right):
    nrows = x.shape[0]
    left = jnp.where(has_left, pltpu.roll(x, 1, 0), 0.0)
    right = jnp.where(has_right, pltpu.roll(x, nrows - 1, 0), 0.0)
    return w[0:1, :] * left + w[1:2, :] * x + w[2:3, :] * right


def _mix_out_kernel(x_ref, g_ref, vb_ref, ch_ref, of_ref, ob_ref, nw_ref, pw_ref, ps_ref, cw_ref,
                    band_ref, cnt_ref, wout_ref, g1_ref, lng_ref, lnb_ref, o_ref, cat_ref,
                    *, mod_row, period, tm):
    has_left, has_right = _row_edge_masks(MIX_ROWS, period)
    g1 = g1_ref[mod_row:mod_row + 1, :]
    blocks = [slice(sb * MIX_ROWS, (sb + 1) * MIX_ROWS) for sb in range(tm // MIX_ROWS)]
    for rows in blocks:
        for h in range(HGRN_HEADS):
            cols = slice(h * HEAD_DIM, (h + 1) * HEAD_DIM)
            o = of_ref[rows, cols] + ob_ref[rows, cols]
            o = o * lax.rsqrt(jnp.mean(o * o, -1, keepdims=True) + EPS) * nw_ref[:, cols]
            cat_ref[rows, cols] = (o * _silu(g_ref[rows, cols])).astype(BF16)
        for gi in range(len(POOL_WINDOWS)):
            cols = slice(gi * POOL_GROUP, (gi + 1) * POOL_GROUP)
            v = vb_ref[rows, cols]
            v_hi, v_mid = _split2(v)
            band = band_ref[gi]
            dlt = (_dot(band, v_hi) + _dot(band, v_mid)) / cnt_ref[gi] - v
            y = _dot(dlt.astype(BF16), pw_ref[gi]) * ps_ref[:, cols]
            cat_ref[rows, D_HGRN + gi * POOL_GROUP:D_HGRN + (gi + 1) * POOL_GROUP] = y.astype(BF16)
        bg = vb_ref[rows, D_POOL:]
        ch = ch_ref[rows, :D_CONV] * ch_ref[rows, D_CONV:]
        cat_ref[rows, D_HGRN + D_POOL:] = (bg * _dwconv3(ch, cw_ref[...], has_left, has_right)).astype(BF16)
    ys = [_dot(cat_ref[rows, :], wout_ref[...]) for rows in blocks]
    for rows, y in zip(blocks, ys):
        r = ALPHA * x_ref[rows, :] + g1 * y
        o_ref[rows, :] = _layer_norm(r) * lng_ref[...] + lnb_ref[...]


def _mix_out(x, p, o_f, o_b, mods, norm_w, pool_w, pool_scale, conv_w, pool_consts, w_out,
             ln_g, ln_b, *, layer, mod_row, period, tm):
    m, d = x.shape
    band, cnt = pool_consts
    const = lambda a: pl.BlockSpec(a.shape, lambda i: (0,) * a.ndim)
    row = lambda width, cb: pl.BlockSpec((tm, width), lambda i: (i, cb))
    vec = lambda a: a.reshape(1, -1)
    args = [
        (x, row(d, 0)),
        (p, row(D_HGRN, 4)),
        (p, row(D_POOL + D_CONV, 5)),
        (p, row(2 * D_CONV, 6)),
        (o_f, row(D_HGRN, 0)),
        (o_b, row(D_HGRN, 0)),
        (vec(norm_w), None), (pool_w, None), (vec(pool_scale), None), (conv_w, None),
        (band, None), (cnt, None),
        (w_out, pl.BlockSpec((None, d, d), lambda i: (layer, 0, 0))),
        (mods, pl.BlockSpec((ADA_ROWS, d), lambda i: (0, 2))),
        (vec(ln_g), None), (vec(ln_b), None),
    ]
    return pl.pallas_call(
        functools.partial(_mix_out_kernel, mod_row=mod_row, period=period, tm=tm),
        grid=(m // tm,),
        in_specs=[s if s is not None else const(a) for a, s in args],
        out_specs=pl.BlockSpec((tm, d), lambda i: (i, 0)),
        out_shape=jax.ShapeDtypeStruct((m, d), F32),
        scratch_shapes=[pltpu.VMEM((tm, d), BF16)],
        compiler_params=_cparams(("arbitrary",)),
        name="mix_out",
    )(*[a for a, _ in args])


def _ffn_kernel(x_ref, sh_ref, sc_ref, g2_ref, wu_ref, wg_ref, cw_ref, cb_ref, wd_ref,
                lng_ref, lnb_ref, o_ref, h_ref, *, mod_row, period, tm):
    j = pl.program_id(1)

    @pl.when(j == 0)
    def _():
        sh = sh_ref[mod_row:mod_row + 1, :]
        sc = sc_ref[mod_row:mod_row + 1, :]
        h_ref[...] = (_layer_norm(x_ref[...]) * (1.0 + sc) + sh).astype(BF16)
        o_ref[...] = jnp.zeros_like(o_ref)

    has_left, has_right = _row_edge_masks(tm, period)
    h = h_ref[...]
    width = wu_ref.shape[1] // FFN_SPLIT
    subs = [slice(a * width, (a + 1) * width) for a in range(FFN_SPLIT)]
    ups = [(_dot(h, wu_ref[:, cols]), _dot(h, wg_ref[:, cols])) for cols in subs]
    for cols, (u, g) in zip(subs, ups):
        act = u * _silu(_dwconv3(g, cw_ref[:, cols], has_left, has_right) + cb_ref[:, cols])
        o_ref[...] += _dot(act.astype(BF16), wd_ref[cols, :])

    @pl.when(j == pl.num_programs(1) - 1)
    def _():
        r = ALPHA * x_ref[...] + g2_ref[mod_row:mod_row + 1, :] * o_ref[...]
        o_ref[...] = _layer_norm(r) * lng_ref[...] + lnb_ref[...]


def _ffn(x, mods, w_up, conv_w, conv_b, w_down, ln_g, ln_b, *, layer, mod_row, period, tm, tf):
    m, d = x.shape
    nf = D_FF // tf
    vec = lambda a: a.reshape(1, -1)
    mod = lambda cb: pl.BlockSpec((ADA_ROWS, d), lambda i, j: (0, cb))
    const = lambda a: pl.BlockSpec(a.shape, lambda i, j: (0,) * a.ndim)
    return pl.pallas_call(
        functools.partial(_ffn_kernel, mod_row=mod_row, period=period, tm=tm),
        grid=(m // tm, nf),
        in_specs=[
            pl.BlockSpec((tm, d), lambda i, j: (i, 0)),
            mod(3), mod(4), mod(5),
            pl.BlockSpec((None, d, tf), lambda i, j: (layer, 0, j)),
            pl.BlockSpec((None, d, tf), lambda i, j: (layer, 0, nf + j)),
            pl.BlockSpec((3, tf), lambda i, j: (0, j)),
            pl.BlockSpec((1, tf), lambda i, j: (0, j)),
            pl.BlockSpec((None, tf, d), lambda i, j: (layer, j, 0)),
            const(vec(ln_g)), const(vec(ln_b)),
        ],
        out_specs=pl.BlockSpec((tm, d), lambda i, j: (i, 0)),
        out_shape=jax.ShapeDtypeStruct((m, d), F32),
        scratch_shapes=[pltpu.VMEM((tm, d), BF16)],
        compiler_params=_cparams(("arbitrary", "arbitrary")),
        name="ffn",
    )(x, mods, mods, mods, w_up, w_up, conv_w, vec(conv_b), w_down, vec(ln_g), vec(ln_b))


def kernel(x, c, ctx, c_ctx, w_ada, b_ada, w_in, lb_logits, hgrn_norm_w, pool_w, pool_scale, conv_w,
           w_out, ln1_g, ln1_b, w_up, ffn_conv_w, ffn_conv_b, w_down, ln2_g, ln2_b):
    assert x.shape[0] == 1 and ctx.shape[0] == 1, "batch size 1 only"
    xs, cs = x[0], ctx[0]
    seq, ctx_len = xs.shape[0], cs.shape[0]

    cc = jnp.concatenate([c, c_ctx[None, :], jnp.zeros((ADA_ROWS - 2, D_MODEL), F32)], 0)
    mods_all = _ada(cc, w_ada, b_ada)

    gla_consts = _gla_constants()
    pool_x = _pool_constants(GRID_W)
    pool_c = _pool_constants(ctx_len)
    zero_state = jnp.zeros((HGRN_HEADS, HEAD_DIM, HEAD_DIM), F32)
    w_in_b, w_out_b = w_in.astype(BF16), w_out.astype(BF16)
    w_up_b, w_down_b, pool_w_b = w_up.astype(BF16), w_down.astype(BF16), pool_w.astype(BF16)

    for l in range(DEPTH):
        ctx_live = l < DEPTH - 1
        mods = mods_all[l]
        pc = _in_proj(cs, mods, w_in_b, layer=l, mod_row=1, tm=ctx_len, tn=1024)
        px = _in_proj(xs, mods, w_in_b, layer=l, mod_row=0, tm=1024, tn=1024)
        ocf, ocb, s_f, s_b = _gla(pc, lb_logits, zero_state, zero_state, gla_consts, layer=l, tb=ctx_len)
        oxf, oxb, _, _ = _gla(px, lb_logits, s_f, s_b, gla_consts, layer=l, tb=2048)
        mix = functools.partial(_mix_out, mods=mods, norm_w=hgrn_norm_w[l], pool_w=pool_w_b[l],
                                pool_scale=pool_scale[l], conv_w=conv_w[l], w_out=w_out_b,
                                ln_g=ln1_g[l], ln_b=ln1_b[l], layer=l)
        ffn = functools.partial(_ffn, mods=mods, w_up=w_up_b, conv_w=ffn_conv_w[l],
                                conv_b=ffn_conv_b[l], w_down=w_down_b, ln_g=ln2_g[l], ln_b=ln2_b[l],
                                layer=l, tf=512)
        xs = mix(xs, px, oxf, oxb, pool_consts=pool_x, mod_row=0, period=GRID_W, tm=MIX_ROWS)
        if ctx_live:
            cs = mix(cs, pc, ocf, ocb, pool_consts=pool_c, mod_row=1, period=ctx_len, tm=ctx_len)
        xs = ffn(xs, mod_row=0, period=GRID_W, tm=512)
        if ctx_live:
            cs = ffn(cs, mod_row=1, period=ctx_len, tm=ctx_len)
    return xs[None]
```

```python
import functools

import jax
import jax.numpy as jnp
import numpy as np
from jax import lax
from jax.experimental import pallas as pl
from jax.experimental.pallas import tpu as pltpu

D_MODEL = 2048
DEPTH = 2
GRID_W = 64
CTX_LEN = 256
D_HGRN = 1024
HGRN_HEADS = 8
HEAD_DIM = D_HGRN // HGRN_HEADS
D_POOL = 512
POOL_WINDOWS = (2, 4, 8, 16)
POOL_GROUP = D_POOL // len(POOL_WINDOWS)
D_CONV = 512
D_FF = 5632
D_IN = 5 * D_HGRN + D_POOL + 3 * D_CONV
ALPHA = (2 * DEPTH) ** 0.25
EPS = 1e-6
F_MIN = 1e-30
F32 = jnp.float32
BF16 = jnp.bfloat16

GLA_CHUNK = 64
GLA_LEVELS = (32, 16, 8, 4, 2, 1)
GLA_UNROLL = 16
GLA_SKEW = 0
MIX_ROWS = 256
FFN_SPLIT = 2
SUBLANES = 8
ADA_ROWS = SUBLANES
VMEM_LIMIT = 56 * 1024 * 1024


def _cparams(sem):
    return pltpu.CompilerParams(dimension_semantics=sem, vmem_limit_bytes=VMEM_LIMIT)


def _silu(z):
    h = 0.5 * z
    return h + h * jnp.tanh(h)


def _layer_norm(x):
    mu = jnp.mean(x, -1, keepdims=True)
    xc = x - mu
    var = jnp.mean(xc * xc, -1, keepdims=True)
    return xc * lax.rsqrt(var + EPS)


def _dot_nt(a, b):
    return lax.dot_general(a, b, (((1,), (1,)), ((), ())), preferred_element_type=F32)


def _dot(a, b):
    return jnp.dot(a, b, preferred_element_type=F32)


def _neg_abs(a):
    bits = lax.bitcast_convert_type(a, jnp.uint32) | jnp.uint32(0x80000000)
    return lax.bitcast_convert_type(bits, F32)


def _split2(a):
    hi = a.astype(BF16)
    mid = (a - hi.astype(F32)).astype(BF16)
    return hi, mid


def _ada_kernel(cc_ref, w_ref, b_ref, o_ref):
    a = _silu(cc_ref[...])
    o_ref[...] = jnp.dot(a, w_ref[...], preferred_element_type=F32,
                         precision=lax.Precision.HIGHEST) + b_ref[...]


def _ada(cc, w_ada, b_ada):
    depth, d, n = w_ada.shape
    tn = 1024
    return pl.pallas_call(
        _ada_kernel,
        grid=(depth, n // tn),
        in_specs=[
            pl.BlockSpec((ADA_ROWS, d), lambda l, j: (0, 0)),
            pl.BlockSpec((None, d, tn), lambda l, j: (l, 0, j)),
            pl.BlockSpec((None, 1, tn), lambda l, j: (l, 0, j)),
        ],
        out_specs=pl.BlockSpec((None, ADA_ROWS, tn), lambda l, j: (l, 0, j)),
        out_shape=jax.ShapeDtypeStruct((depth, ADA_ROWS, n), F32),
        compiler_params=_cparams(("arbitrary", "arbitrary")),
        name="ada",
    )(cc, w_ada, b_ada.reshape(depth, 1, n))


def _in_proj_kernel(x_ref, sh_ref, sc_ref, w_ref, o_ref, h_ref, *, mod_row):
    @pl.when(pl.program_id(1) == 0)
    def _():
        sh = sh_ref[mod_row:mod_row + 1, :]
        sc = sc_ref[mod_row:mod_row + 1, :]
        h_ref[...] = (_layer_norm(x_ref[...]) * (1.0 + sc) + sh).astype(BF16)

    o_ref[...] = _dot(h_ref[...], w_ref[...])


def _in_proj(x, mods, w, *, layer, mod_row, tm, tn):
    m, d = x.shape
    n = w.shape[2]
    return pl.pallas_call(
        functools.partial(_in_proj_kernel, mod_row=mod_row),
        grid=(m // tm, n // tn),
        in_specs=[
            pl.BlockSpec((tm, d), lambda i, j: (i, 0)),
            pl.BlockSpec((ADA_ROWS, d), lambda i, j: (0, 0)),
            pl.BlockSpec((ADA_ROWS, d), lambda i, j: (0, 1)),
            pl.BlockSpec((None, d, tn), lambda i, j: (layer, 0, j)),
        ],
        out_specs=pl.BlockSpec((tm, tn), lambda i, j: (i, j)),
        out_shape=jax.ShapeDtypeStruct((m, n), F32),
        scratch_shapes=[pltpu.VMEM((tm, d), BF16)],
        compiler_params=_cparams(("arbitrary", "arbitrary")),
        name="in_proj",
    )(x, mods, mods, w)


def _gla_constants():
    n = GLA_CHUNK
    t = np.arange(n)[:, None]
    r = np.arange(n)[None, :]
    out = []
    for fwd in (True, False):
        masks = []
        for c in GLA_LEVELS:
            mid = (t // (2 * c)) * (2 * c) + c
            mid_s = (r // (2 * c)) * (2 * c) + c
            same = (t // (2 * c)) == (r // (2 * c))
            masks.append(same & ((t >= mid) & (r < mid_s) if fwd else (t < mid) & (r >= mid_s)))
        mid = (t // 4) * 4 + 2
        if fwd:
            cum = r <= t
            lvl2 = ((t >= mid) & (r >= mid) & (r <= t)) | ((t < mid) & (r > t) & (r <= mid - 1))
        else:
            cum = r >= t
            lvl2 = ((t >= mid) & (r >= mid) & (r <= t - 1)) | ((t < mid) & (r >= t) & (r <= mid - 1))
        w = np.concatenate([cum, lvl2], 0).astype(np.float32)
        out.append((jnp.asarray(w, BF16), jnp.asarray(np.stack(masks), F32)))
    return out


def _lower_bound(logits, layer):
    rows = [logits[j:j + 1, :] for j in range(DEPTH)]
    mx = functools.reduce(jnp.maximum, rows)
    es = [jnp.exp(r - mx) for r in rows]
    tot = functools.reduce(jnp.add, es)
    acc = jnp.zeros_like(mx)
    for j in range(1, layer + 1):
        acc = acc + es[j] / tot
    return acc


def _boundary_rows(b, c, fwd):
    parts = []
    for p in range(GLA_CHUNK // (2 * c)):
        r = p * 2 * c + (c - 1 if fwd else c)
        parts.append(jnp.broadcast_to(b[r:r + 1, :], (2 * c, b.shape[1])))
    return parts[0] if len(parts) == 1 else jnp.concatenate(parts, 0)


def _gla_load(s):
    n = GLA_CHUNK
    s["rows"] = rows = pl.ds(pl.multiple_of(s["c"] * n, n), n)
    half = 0.5 * (1.0 - s["lb"])
    s["v"] = s["v_ref"][rows, :]
    s["q"] = _silu(s["q_ref"][rows, :])
    ht = half * jnp.tanh(0.5 * s["z_ref"][rows, :])
    s["f"] = jnp.maximum((s["lb"] + half) + ht, F_MIN)
    s["k"] = half - ht
    s["g2"] = _split2(jnp.log2(s["f"]))


def _gla_cumsum(s):
    w = s["w_ref"][...]
    s["d"] = _dot(w, s["g2"][0]) + _dot(w, s["g2"][1])


def _gla_decays(s):
    n, fwd = GLA_CHUNK, s["fwd"]
    b = s["d"][0:n]
    b_end = b[n - 1:n, :] if fwd else b[0:1, :]
    s["q_in"] = (s["q"] * jnp.exp2(b)).astype(BF16)
    s["k_out"] = (s["k"] * jnp.exp2(b_end - b)).astype(BF16)
    s["d_all"] = jnp.exp2(b_end)
    es = []
    for c in GLA_LEVELS:
        if c >= 4:
            es.append(jnp.exp2(_neg_abs(b - _boundary_rows(b, c, fwd))))
        elif c == 2:
            es.append(jnp.exp2(s["d"][n:2 * n]))
        else:
            row = lax.broadcasted_iota(jnp.int32, (n, 1), 0)
            es.append(jnp.where((row & 1) == (1 if fwd else 0), s["f"], 1.0))
    s["es"] = [e.astype(BF16) for e in es]
    s["qk16"] = s["q"].astype(BF16), s["k"].astype(BF16)
    s["ps"] = []


def _gla_level(l, s):
    e = s["es"][l]
    q16, k16 = s["qk16"]
    s["ps"].append(_dot_nt(q16 * e, k16 * e))


def _level_selects_band(c, band, fwd):
    if c < SUBLANES:
        return True
    in_later_half = ((band * SUBLANES) // c) % 2 == 1
    return in_later_half if fwd else not in_later_half


def _gla_intra(s):
    bands = []
    for band in range(GLA_CHUNK // SUBLANES):
        rows = slice(band * SUBLANES, (band + 1) * SUBLANES)
        acc = None
        for l, c in enumerate(GLA_LEVELS):
            if _level_selects_band(c, band, s["fwd"]):
                p = s["m_ref"][l, rows, :] * s["ps"][l][rows, :]
                acc = p if acc is None else acc + p
        bands.append(acc)
    a = jnp.concatenate(bands, 0)
    o_self = jnp.sum(s["q"] * s["k"], -1, keepdims=True) * s["v"]
    s["o"] = o_self + _dot(a.astype(BF16), s["v"].astype(BF16))
    s["upd"] = _dot(s["v"].T.astype(BF16), s["k_out"])


def _gla_state(s):
    st = s["st"][0]
    s["o_ref"][s["rows"], :] = s["o"] + _dot(s["q_in"], st.T.astype(BF16))
    s["st"][0] = s["d_all"] * st + s["upd"]


_GLA_STAGES = ([_gla_load, _gla_cumsum, _gla_decays]
               + [functools.partial(_gla_level, l) for l in range(len(GLA_LEVELS))]
               + [_gla_intra, _gla_state])


def _gla_streams(groups):
    nst = len(_GLA_STAGES)
    for t in range(nst + GLA_SKEW * (len(groups) - 1)):
        for g, streams in enumerate(groups):
            k = t - GLA_SKEW * g
            if 0 <= k < nst:
                for s in streams:
                    _GLA_STAGES[k](s)


def _gla_kernel(qf_ref, vf_ref, zf_ref, qb_ref, vb_ref, zb_ref, lb_ref, s0f_ref, s0b_ref,
                wf_ref, mf_ref, wb_ref, mb_ref,
                of_ref, ob_ref, sf_ref, sb_ref, stf, stb, *, layer, nchunks):
    step = pl.program_id(1)

    @pl.when(step == 0)
    def _():
        stf[...] = s0f_ref[...]
        stb[...] = s0b_ref[...]

    lbf = _lower_bound(lb_ref[0], layer)
    lbb = _lower_bound(lb_ref[1], layer)

    unroll = min(GLA_UNROLL, nchunks)
    assert nchunks % unroll == 0

    def body(ci, carry):
        st_f, st_b = [stf[...]], [stb[...]]
        groups = []
        for u in range(unroll):
            c = ci * unroll + u
            groups.append([
                dict(q_ref=qf_ref, v_ref=vf_ref, z_ref=zf_ref, o_ref=of_ref, w_ref=wf_ref,
                     m_ref=mf_ref, lb=lbf, c=c, fwd=True, st=st_f),
                dict(q_ref=qb_ref, v_ref=vb_ref, z_ref=zb_ref, o_ref=ob_ref, w_ref=wb_ref,
                     m_ref=mb_ref, lb=lbb, c=nchunks - 1 - c, fwd=False, st=st_b)])
        _gla_streams(groups)
        stf[...] = st_f[0]
        stb[...] = st_b[0]
        return carry

    lax.fori_loop(0, nchunks // unroll, body, 0)

    @pl.when(step == pl.num_programs(1) - 1)
    def _():
        sf_ref[...] = stf[...]
        sb_ref[...] = stb[...]


def _gla(p, lb_logits, s0f, s0b, consts, *, layer, tb):
    t = p.shape[0]
    nb = t // tb
    hd, h = HEAD_DIM, HGRN_HEADS
    (wf, mf), (wb, mb) = consts

    def col(group, rev):
        if rev:
            return pl.BlockSpec((tb, hd), lambda hh, n: (nb - 1 - n, group * h + hh))
        return pl.BlockSpec((tb, hd), lambda hh, n: (n, group * h + hh))

    state_spec = pl.BlockSpec((None, hd, hd), lambda hh, n: (hh, 0, 0))
    const2 = lambda a: pl.BlockSpec(a.shape, lambda hh, n: (0,) * a.ndim)
    return pl.pallas_call(
        functools.partial(_gla_kernel, layer=layer, nchunks=tb // GLA_CHUNK),
        grid=(h, nb),
        in_specs=[
            col(0, False), col(1, False), col(2, False),
            col(0, True), col(1, True), col(3, True),
            pl.BlockSpec((2, DEPTH, hd), lambda hh, n: (0, 0, hh)),
            state_spec, state_spec,
            const2(wf), const2(mf), const2(wb), const2(mb),
        ],
        out_specs=[
            pl.BlockSpec((tb, hd), lambda hh, n: (n, hh)),
            pl.BlockSpec((tb, hd), lambda hh, n: (nb - 1 - n, hh)),
            state_spec, state_spec,
        ],
        out_shape=[
            jax.ShapeDtypeStruct((t, D_HGRN), F32),
            jax.ShapeDtypeStruct((t, D_HGRN), F32),
            jax.ShapeDtypeStruct((h, hd, hd), F32),
            jax.ShapeDtypeStruct((h, hd, hd), F32),
        ],
        scratch_shapes=[pltpu.VMEM((hd, hd), F32), pltpu.VMEM((hd, hd), F32)],
        compiler_params=_cparams(("arbitrary", "arbitrary")),
        name="gla",
    )(p, p, p, p, p, p, lb_logits, s0f, s0b, wf, mf, wb, mb)


def _pool_constants(period):
    n = MIX_ROWS
    t = np.arange(n)
    tau = t % period
    base = t - tau
    bands, inv_cnts = [], []
    for w in POOL_WINDOWS:
        lo = np.maximum(tau - w // 2, 0)
        hi = np.minimum(tau + w // 2, period)
        s = np.arange(n)[None, :]
        bands.append(((s >= (base + lo)[:, None]) & (s < (base + hi)[:, None])).astype(np.float32))
        inv_cnts.append(np.broadcast_to((1.0 / (hi - lo))[:, None], (n, POOL_GROUP)))
    return jnp.asarray(np.stack(bands), BF16), jnp.asarray(np.stack(inv_cnts), F32)


def _zero_edge_row(a, period, first):
    n, c = a.shape
    a3 = a.reshape(n // period, period, c)
    row = lax.broadcasted_iota(jnp.int32, (1, SUBLANES, 1), 1)
    if first:
        band = jnp.where(row == 0, 0.0, a3[:, :SUBLANES])
        out = jnp.concatenate([band, a3[:, SUBLANES:]], 1)
    else:
        band = jnp.where(row == SUBLANES - 1, 0.0, a3[:, period - SUBLANES:])
        out = jnp.concatenate([a3[:, :period - SUBLANES], band], 1)
    return out.reshape(n, c)


def _dwconv3(x, w, period):
    nrows = x.shape[0]
    left = _zero_edge_row(pltpu.roll(x, 1, 0), period, True)
    right = _zero_edge_row(pltpu.roll(x, nrows - 1, 0), period, False)
    return w[0:1, :] * left + w[1:2, :] * x + w[2:3, :] * right


def _mix_out_kernel(x_ref, g_ref, vb_ref, ch_ref, of_ref, ob_ref, nw_ref, pw_ref, ps_ref, cw_ref,
                    band_ref, inv_cnt_ref, wout_ref, g1_ref, lng_ref, lnb_ref, o_ref, cat_ref,
                    *, mod_row, period, tm):
    g1 = g1_ref[mod_row:mod_row + 1, :]
    blocks = [slice(sb * MIX_ROWS, (sb + 1) * MIX_ROWS) for sb in range(tm // MIX_ROWS)]
    for rows in blocks:
        for h in range(HGRN_HEADS):
            cols = slice(h * HEAD_DIM, (h + 1) * HEAD_DIM)
            o = of_ref[rows, cols] + ob_ref[rows, cols]
            o = o * lax.rsqrt(jnp.mean(o * o, -1, keepdims=True) + EPS) * nw_ref[:, cols]
            cat_ref[rows, cols] = (o * _silu(g_ref[rows, cols])).astype(BF16)
        for gi in range(len(POOL_WINDOWS)):
            cols = slice(gi * POOL_GROUP, (gi + 1) * POOL_GROUP)
            v = vb_ref[rows, cols]
            v_hi, v_mid = _split2(v)
            band = band_ref[gi]
            dlt = (_dot(band, v_hi) + _dot(band, v_mid)) * inv_cnt_ref[gi] - v
            y = _dot(dlt.astype(BF16), pw_ref[gi]) * ps_ref[:, cols]
            cat_ref[rows, D_HGRN + gi * POOL_GROUP:D_HGRN + (gi + 1) * POOL_GROUP] = y.astype(BF16)
        bg = vb_ref[rows, D_POOL:]
        ch = ch_ref[rows, :D_CONV] * ch_ref[rows, D_CONV:]
        cat_ref[rows, D_HGRN + D_POOL:] = (bg * _dwconv3(ch, cw_ref[...], period)).astype(BF16)
    ys = [_dot(cat_ref[rows, :], wout_ref[...]) for rows in blocks]
    for rows, y in zip(blocks, ys):
        r = ALPHA * x_ref[rows, :] + g1 * y
        o_ref[rows, :] = _layer_norm(r) * lng_ref[...] + lnb_ref[...]


def _mix_out(x, p, o_f, o_b, mods, norm_w, pool_w, pool_scale, conv_w, pool_consts, w_out,
             ln_g, ln_b, *, layer, mod_row, period, tm):
    m, d = x.shape
    band, inv_cnt = pool_consts
    const = lambda a: pl.BlockSpec(a.shape, lambda i: (0,) * a.ndim)
    row = lambda width, cb: pl.BlockSpec((tm, width), lambda i: (i, cb))
    vec = lambda a: a.reshape(1, -1)
    args = [
        (x, row(d, 0)),
        (p, row(D_HGRN, 4)),
        (p, row(D_POOL + D_CONV, 5)),
        (p, row(2 * D_CONV, 6)),
        (o_f, row(D_HGRN, 0)),
        (o_b, row(D_HGRN, 0)),
        (vec(norm_w), None), (pool_w, None), (vec(pool_scale), None), (conv_w, None),
        (band, None), (inv_cnt, None),
        (w_out, pl.BlockSpec((None, d, d), lambda i: (layer, 0, 0))),
        (mods, pl.BlockSpec((ADA_ROWS, d), lambda i: (0, 2))),
        (vec(ln_g), None), (vec(ln_b), None),
    ]
    return pl.pallas_call(
        functools.partial(_mix_out_kernel, mod_row=mod_row, period=period, tm=tm),
        grid=(m // tm,),
        in_specs=[s if s is not None else const(a) for a, s in args],
        out_specs=pl.BlockSpec((tm, d), lambda i: (i, 0)),
        out_shape=jax.ShapeDtypeStruct((m, d), F32),
        scratch_shapes=[pltpu.VMEM((tm, d), BF16)],
        compiler_params=_cparams(("arbitrary",)),
        name="mix_out",
    )(*[a for a, _ in args])


def _ffn_kernel(x_ref, sh_ref, sc_ref, g2_ref, wu_ref, wg_ref, cw_ref, cb_ref, wd_ref,
                lng_ref, lnb_ref, o_ref, h_ref, *, mod_row, period, tm):
    j = pl.program_id(1)

    @pl.when(j == 0)
    def _():
        sh = sh_ref[mod_row:mod_row + 1, :]
        sc = sc_ref[mod_row:mod_row + 1, :]
        h_ref[...] = (_layer_norm(x_ref[...]) * (1.0 + sc) + sh).astype(BF16)
        o_ref[...] = jnp.zeros_like(o_ref)

    h = h_ref[...]
    width = wu_ref.shape[1] // FFN_SPLIT
    subs = [slice(a * width, (a + 1) * width) for a in range(FFN_SPLIT)]
    ups = [(_dot(h, wu_ref[:, cols]), _dot(h, wg_ref[:, cols])) for cols in subs]
    for cols, (u, g) in zip(subs, ups):
        hz = _dwconv3(g, 0.5 * cw_ref[:, cols], period) + 0.5 * cb_ref[:, cols]
        act = u * (hz + hz * jnp.tanh(hz))
        o_ref[...] += _dot(act.astype(BF16), wd_ref[cols, :])

    @pl.when(j == pl.num_programs(1) - 1)
    def _():
        r = ALPHA * x_ref[...] + g2_ref[mod_row:mod_row + 1, :] * o_ref[...]
        o_ref[...] = _layer_norm(r) * lng_ref[...] + lnb_ref[...]


def _ffn(x, mods, w_up, conv_w, conv_b, w_down, ln_g, ln_b, *, layer, mod_row, period, tm, tf):
    m, d = x.shape
    nf = D_FF // tf
    vec = lambda a: a.reshape(1, -1)
    mod = lambda cb: pl.BlockSpec((ADA_ROWS, d), lambda i, j: (0, cb))
    const = lambda a: pl.BlockSpec(a.shape, lambda i, j: (0,) * a.ndim)
    return pl.pallas_call(
        functools.partial(_ffn_kernel, mod_row=mod_row, period=period, tm=tm),
        grid=(m // tm, nf),
        in_specs=[
            pl.BlockSpec((tm, d), lambda i, j: (i, 0)),
            mod(3), mod(4), mod(5),
            pl.BlockSpec((None, d, tf), lambda i, j: (layer, 0, j)),
            pl.BlockSpec((None, d, tf), lambda i, j: (layer, 0, nf + j)),
            pl.BlockSpec((3, tf), lambda i, j: (0, j)),
            pl.BlockSpec((1, tf), lambda i, j: (0, j)),
            pl.BlockSpec((None, tf, d), lambda i, j: (layer, j, 0)),
            const(vec(ln_g)), const(vec(ln_b)),
        ],
        out_specs=pl.BlockSpec((tm, d), lambda i, j: (i, 0)),
        out_shape=jax.ShapeDtypeStruct((m, d), F32),
        scratch_shapes=[pltpu.VMEM((tm, d), BF16)],
        compiler_params=_cparams(("arbitrary", "arbitrary")),
        name="ffn",
    )(x, mods, mods, mods, w_up, w_up, conv_w, vec(conv_b), w_down, vec(ln_g), vec(ln_b))


def kernel(x, c, ctx, c_ctx, w_ada, b_ada, w_in, lb_logits, hgrn_norm_w, pool_w, pool_scale, conv_w,
           w_out, ln1_g, ln1_b, w_up, ffn_conv_w, ffn_conv_b, w_down, ln2_g, ln2_b):
    assert x.shape[0] == 1 and ctx.shape[0] == 1, "batch size 1 only"
    xs, cs = x[0], ctx[0]
    seq, ctx_len = xs.shape[0], cs.shape[0]

    cc = jnp.concatenate([c, c_ctx[None, :], jnp.zeros((ADA_ROWS - 2, D_MODEL), F32)], 0)
    mods_all = _ada(cc, w_ada, b_ada)

    gla_consts = _gla_constants()
    pool_x = _pool_constants(GRID_W)
    pool_c = _pool_constants(ctx_len)
    zero_state = jnp.zeros((HGRN_HEADS, HEAD_DIM, HEAD_DIM), F32)
    w_in_b, w_out_b = w_in.astype(BF16), w_out.astype(BF16)
    w_up_b, w_down_b, pool_w_b = w_up.astype(BF16), w_down.astype(BF16), pool_w.astype(BF16)

    for l in range(DEPTH):
        ctx_live = l < DEPTH - 1
        mods = mods_all[l]
        pc = _in_proj(cs, mods, w_in_b, layer=l, mod_row=1, tm=ctx_len, tn=1024)
        px = _in_proj(xs, mods, w_in_b, layer=l, mod_row=0, tm=1024, tn=1024)
        ocf, ocb, s_f, s_b = _gla(pc, lb_logits, zero_state, zero_state, gla_consts, layer=l, tb=ctx_len)
        oxf, oxb, _, _ = _gla(px, lb_logits, s_f, s_b, gla_consts, layer=l, tb=2048)
        mix = functools.partial(_mix_out, mods=mods, norm_w=hgrn_norm_w[l], pool_w=pool_w_b[l],
                                pool_scale=pool_scale[l], conv_w=conv_w[l], w_out=w_out_b,
                                ln_g=ln1_g[l], ln_b=ln1_b[l], layer=l)
        ffn = functools.partial(_ffn, mods=mods, w_up=w_up_b, conv_w=ffn_conv_w[l],
                                conv_b=ffn_conv_b[l], w_down=w_down_b, ln_g=ln2_g[l], ln_b=ln2_b[l],
                                layer=l, tf=512)
        xs = mix(xs, px, oxf, oxb, pool_consts=pool_x, mod_row=0, period=GRID_W, tm=MIX_ROWS)
        if ctx_live:
            cs = mix(cs, pc, ocf, ocb, pool_consts=pool_c, mod_row=1, period=ctx_len, tm=ctx_len)
        xs = ffn(xs, mod_row=0, period=GRID_W, tm=512)
        if ctx_live:
            cs = ffn(cs, mod_row=1, period=ctx_len, tm=ctx_len)
    return xs[None]
```

```python
import functools

import jax
import jax.numpy as jnp
import numpy as np
from jax import lax
from jax.experimental import pallas as pl
from jax.experimental.pallas import tpu as pltpu

D_MODEL = 2048
DEPTH = 2
GRID_W = 64
CTX_LEN = 256
D_HGRN = 1024
HGRN_HEADS = 8
HEAD_DIM = D_HGRN // HGRN_HEADS
D_POOL = 512
POOL_WINDOWS = (2, 4, 8, 16)
POOL_GROUP = D_POOL // len(POOL_WINDOWS)
D_CONV = 512
D_FF = 5632
D_IN = 5 * D_HGRN + D_POOL + 3 * D_CONV
ALPHA = (2 * DEPTH) ** 0.25
EPS = 1e-6
F_MIN = 1e-30
F32 = jnp.float32
BF16 = jnp.bfloat16

GLA_CHUNK = 64
GLA_LEVELS = (32, 16, 8, 4, 2, 1)
GLA_UNROLL = 16
GLA_SKEW = 0
MIX_ROWS = 256
FFN_SPLIT = 2
SUBLANES = 8
ADA_ROWS = SUBLANES
VMEM_LIMIT = 56 * 1024 * 1024


def _cparams(sem):
    return pltpu.CompilerParams(dimension_semantics=sem, vmem_limit_bytes=VMEM_LIMIT)


def _silu(z):
    h = 0.5 * z
    return h + h * jnp.tanh(h)


def _layer_norm(x):
    mu = jnp.mean(x, -1, keepdims=True)
    xc = x - mu
    var = jnp.mean(xc * xc, -1, keepdims=True)
    return xc * lax.rsqrt(var + EPS)


def _dot_nt(a, b):
    return lax.dot_general(a, b, (((1,), (1,)), ((), ())), preferred_element_type=F32)


def _dot(a, b):
    return jnp.dot(a, b, preferred_element_type=F32)


def _neg_abs(a):
    bits = lax.bitcast_convert_type(a, jnp.uint32) | jnp.uint32(0x80000000)
    return lax.bitcast_convert_type(bits, F32)


def _split2(a):
    hi = a.astype(BF16)
    mid = (a - hi.astype(F32)).astype(BF16)
    return hi, mid


def _ada_kernel(cc_ref, w_ref, b_ref, o_ref):
    a = _silu(cc_ref[...])
    o_ref[...] = jnp.dot(a, w_ref[...], preferred_element_type=F32,
                         precision=lax.Precision.HIGHEST) + b_ref[...]


def _ada(cc, w_ada, b_ada):
    depth, d, n = w_ada.shape
    tn = 1024
    return pl.pallas_call(
        _ada_kernel,
        grid=(depth, n // tn),
        in_specs=[
            pl.BlockSpec((ADA_ROWS, d), lambda l, j: (0, 0)),
            pl.BlockSpec((None, d, tn), lambda l, j: (l, 0, j)),
            pl.BlockSpec((None, 1, tn), lambda l, j: (l, 0, j)),
        ],
        out_specs=pl.BlockSpec((None, ADA_ROWS, tn), lambda l, j: (l, 0, j)),
        out_shape=jax.ShapeDtypeStruct((depth, ADA_ROWS, n), F32),
        compiler_params=_cparams(("arbitrary", "arbitrary")),
        name="ada",
    )(cc, w_ada, b_ada.reshape(depth, 1, n))


def _in_proj_kernel(x_ref, sh_ref, sc_ref, w_ref, o_ref, h_ref, *, mod_row):
    @pl.when(pl.program_id(1) == 0)
    def _():
        sh = sh_ref[mod_row:mod_row + 1, :]
        sc = sc_ref[mod_row:mod_row + 1, :]
        h_ref[...] = (_layer_norm(x_ref[...]) * (1.0 + sc) + sh).astype(BF16)

    o_ref[...] = _dot(h_ref[...], w_ref[...])


def _in_proj(x, mods, w, *, layer, mod_row, tm, tn):
    m, d = x.shape
    n = w.shape[2]
    return pl.pallas_call(
        functools.partial(_in_proj_kernel, mod_row=mod_row),
        grid=(m // tm, n // tn),
        in_specs=[
            pl.BlockSpec((tm, d), lambda i, j: (i, 0)),
            pl.BlockSpec((ADA_ROWS, d), lambda i, j: (0, 0)),
            pl.BlockSpec((ADA_ROWS, d), lambda i, j: (0, 1)),
            pl.BlockSpec((None, d, tn), lambda i, j: (layer, 0, j)),
        ],
        out_specs=pl.BlockSpec((tm, tn), lambda i, j: (i, j)),
        out_shape=jax.ShapeDtypeStruct((m, n), F32),
        scratch_shapes=[pltpu.VMEM((tm, d), BF16)],
        compiler_params=_cparams(("arbitrary", "arbitrary")),
        name="in_proj",
    )(x, mods, mods, w)


def _gla_constants():
    n = GLA_CHUNK
    t = np.arange(n)[:, None]
    r = np.arange(n)[None, :]
    out = []
    for fwd in (True, False):
        masks = []
        for c in GLA_LEVELS:
            mid = (t // (2 * c)) * (2 * c) + c
            mid_s = (r // (2 * c)) * (2 * c) + c
            same = (t // (2 * c)) == (r // (2 * c))
            masks.append(same & ((t >= mid) & (r < mid_s) if fwd else (t < mid) & (r >= mid_s)))
        mid = (t // 4) * 4 + 2
        if fwd:
            cum = r <= t
            lvl2 = ((t >= mid) & (r >= mid) & (r <= t)) | ((t < mid) & (r > t) & (r <= mid - 1))
        else:
            cum = r >= t
            lvl2 = ((t >= mid) & (r >= mid) & (r <= t - 1)) | ((t < mid) & (r >= t) & (r <= mid - 1))
        w = np.concatenate([cum, lvl2], 0).astype(np.float32)
        out.append((jnp.asarray(w, BF16), jnp.asarray(np.stack(masks), F32)))
    return out


def _lower_bound(logits, layer):
    rows = [logits[j:j + 1, :] for j in range(DEPTH)]
    mx = functools.reduce(jnp.maximum, rows)
    es = [jnp.exp(r - mx) for r in rows]
    tot = functools.reduce(jnp.add, es)
    acc = jnp.zeros_like(mx)
    for j in range(1, layer + 1):
        acc = acc + es[j] / tot
    return acc


def _boundary_decay(b, c, fwd):
    bands, boundary = [], {}
    for band in range(GLA_CHUNK // SUBLANES):
        t0 = band * SUBLANES
        r = (t0 // (2 * c)) * 2 * c + (c - 1 if fwd else c)
        if r not in boundary:
            boundary[r] = jnp.broadcast_to(b[r:r + 1, :], (SUBLANES, b.shape[1]))
        rows = b[t0:t0 + SUBLANES]
        if c < SUBLANES:
            bands.append(_neg_abs(rows - boundary[r]))
        elif ((t0 // c) % 2 == 1) == fwd:
            bands.append(rows - boundary[r])
        else:
            bands.append(boundary[r] - rows)
    return jnp.concatenate(bands, 0)


def _gla_load(s):
    n = GLA_CHUNK
    s["rows"] = rows = pl.ds(pl.multiple_of(s["c"] * n, n), n)
    half = 0.5 * (1.0 - s["lb"])
    s["v"] = s["v_ref"][rows, :]
    s["q"] = _silu(s["q_ref"][rows, :])
    ht = half * jnp.tanh(0.5 * s["z_ref"][rows, :])
    s["f"] = jnp.maximum((s["lb"] + half) + ht, F_MIN)
    s["k"] = half - ht
    s["g2"] = _split2(jnp.log2(s["f"]))


def _gla_cumsum(s):
    w = s["w_ref"][...]
    s["d"] = _dot(w, s["g2"][0]) + _dot(w, s["g2"][1])


def _gla_decays(s):
    n, fwd = GLA_CHUNK, s["fwd"]
    b = s["d"][0:n]
    b_end = b[n - 1:n, :] if fwd else b[0:1, :]
    s["q_in"] = (s["q"] * jnp.exp2(b)).astype(BF16)
    s["k_out"] = (s["k"] * jnp.exp2(b_end - b)).astype(BF16)
    s["d_all"] = jnp.exp2(b_end)
    es = []
    for c in GLA_LEVELS:
        if c >= 4:
            es.append(jnp.exp2(_boundary_decay(b, c, fwd)))
        elif c == 2:
            es.append(jnp.exp2(s["d"][n:2 * n]))
        else:
            row = lax.broadcasted_iota(jnp.int32, (n, 1), 0)
            es.append(jnp.where((row & 1) == (1 if fwd else 0), s["f"], 1.0))
    s["es"] = [e.astype(BF16) for e in es]
    s["qk16"] = s["q"].astype(BF16), s["k"].astype(BF16)
    s["ps"] = []


def _gla_level(l, s):
    e = s["es"][l]
    q16, k16 = s["qk16"]
    s["ps"].append(_dot_nt(q16 * e, k16 * e))


def _level_selects_band(c, band, fwd):
    if c < SUBLANES:
        return True
    in_later_half = ((band * SUBLANES) // c) % 2 == 1
    return in_later_half if fwd else not in_later_half


def _gla_intra(s):
    bands = []
    for band in range(GLA_CHUNK // SUBLANES):
        rows = slice(band * SUBLANES, (band + 1) * SUBLANES)
        acc = None
        for l, c in enumerate(GLA_LEVELS):
            if _level_selects_band(c, band, s["fwd"]):
                p = s["m_ref"][l, rows, :] * s["ps"][l][rows, :]
                acc = p if acc is None else acc + p
        bands.append(acc)
    a = jnp.concatenate(bands, 0)
    o_self = jnp.sum(s["q"] * s["k"], -1, keepdims=True) * s["v"]
    s["o"] = o_self + _dot(a.astype(BF16), s["v"].astype(BF16))
    s["upd"] = lax.dot_general(s["v"].astype(BF16), s["k_out"], (((0,), (0,)), ((), ())),
                               preferred_element_type=F32)


def _gla_state(s):
    st = s["st"][0]
    s["o_ref"][s["rows"], :] = s["o"] + _dot(s["q_in"], st.T.astype(BF16))
    s["st"][0] = s["d_all"] * st + s["upd"]


_GLA_STAGES = ([_gla_load, _gla_cumsum, _gla_decays]
               + [functools.partial(_gla_level, l) for l in range(len(GLA_LEVELS))]
               + [_gla_intra, _gla_state])


def _gla_streams(groups):
    nst = len(_GLA_STAGES)
    for t in range(nst + GLA_SKEW * (len(groups) - 1)):
        for g, streams in enumerate(groups):
            k = t - GLA_SKEW * g
            if 0 <= k < nst:
                for s in streams:
                    _GLA_STAGES[k](s)


def _gla_kernel(qf_ref, vf_ref, zf_ref, qb_ref, vb_ref, zb_ref, lb_ref, s0f_ref, s0b_ref,
                wf_ref, mf_ref, wb_ref, mb_ref,
                of_ref, ob_ref, sf_ref, sb_ref, stf, stb, *, layer, nchunks):
    step = pl.program_id(1)

    @pl.when(step == 0)
    def _():
        stf[...] = s0f_ref[...]
        stb[...] = s0b_ref[...]

    lbf = _lower_bound(lb_ref[0], layer)
    lbb = _lower_bound(lb_ref[1], layer)

    unroll = min(GLA_UNROLL, nchunks)
    assert nchunks % unroll == 0

    def body(ci, carry):
        st_f, st_b = [stf[...]], [stb[...]]
        groups = []
        for u in range(unroll):
            c = ci * unroll + u
            groups.append([
                dict(q_ref=qf_ref, v_ref=vf_ref, z_ref=zf_ref, o_ref=of_ref, w_ref=wf_ref,
                     m_ref=mf_ref, lb=lbf, c=c, fwd=True, st=st_f),
                dict(q_ref=qb_ref, v_ref=vb_ref, z_ref=zb_ref, o_ref=ob_ref, w_ref=wb_ref,
                     m_ref=mb_ref, lb=lbb, c=nchunks - 1 - c, fwd=False, st=st_b)])
        _gla_streams(groups)
        stf[...] = st_f[0]
        stb[...] = st_b[0]
        return carry

    lax.fori_loop(0, nchunks // unroll, body, 0)

    @pl.when(step == pl.num_programs(1) - 1)
    def _():
        sf_ref[...] = stf[...]
        sb_ref[...] = stb[...]


def _gla(p, lb_logits, s0f, s0b, consts, *, layer, tb):
    t = p.shape[0]
    nb = t // tb
    hd, h = HEAD_DIM, HGRN_HEADS
    (wf, mf), (wb, mb) = consts

    def col(group, rev):
        if rev:
            return pl.BlockSpec((tb, hd), lambda hh, n: (nb - 1 - n, group * h + hh))
        return pl.BlockSpec((tb, hd), lambda hh, n: (n, group * h + hh))

    state_spec = pl.BlockSpec((None, hd, hd), lambda hh, n: (hh, 0, 0))
    const2 = lambda a: pl.BlockSpec(a.shape, lambda hh, n: (0,) * a.ndim)
    return pl.pallas_call(
        functools.partial(_gla_kernel, layer=layer, nchunks=tb // GLA_CHUNK),
        grid=(h, nb),
        in_specs=[
            col(0, False), col(1, False), col(2, False),
            col(0, True), col(1, True), col(3, True),
            pl.BlockSpec((2, DEPTH, hd), lambda hh, n: (0, 0, hh)),
            state_spec, state_spec,
            const2(wf), const2(mf), const2(wb), const2(mb),
        ],
        out_specs=[
            pl.BlockSpec((tb, hd), lambda hh, n: (n, hh)),
            pl.BlockSpec((tb, hd), lambda hh, n: (nb - 1 - n, hh)),
            state_spec, state_spec,
        ],
        out_shape=[
            jax.ShapeDtypeStruct((t, D_HGRN), F32),
            jax.ShapeDtypeStruct((t, D_HGRN), F32),
            jax.ShapeDtypeStruct((h, hd, hd), F32),
            jax.ShapeDtypeStruct((h, hd, hd), F32),
        ],
        scratch_shapes=[pltpu.VMEM((hd, hd), F32), pltpu.VMEM((hd, hd), F32)],
        compiler_params=_cparams(("arbitrary", "arbitrary")),
        name="gla",
    )(p, p, p, p, p, p, lb_logits, s0f, s0b, wf, mf, wb, mb)


def _pool_constants(period):
    n = MIX_ROWS
    t = np.arange(n)
    tau = t % period
    base = t - tau
    bands, inv_cnts = [], []
    for w in POOL_WINDOWS:
        lo = np.maximum(tau - w // 2, 0)
        hi = np.minimum(tau + w // 2, period)
        s = np.arange(n)[None, :]
        bands.append(((s >= (base + lo)[:, None]) & (s < (base + hi)[:, None])).astype(np.float32))
        inv_cnts.append(np.broadcast_to((1.0 / (hi - lo))[:, None], (n, POOL_GROUP)))
    return jnp.asarray(np.stack(bands), BF16), jnp.asarray(np.stack(inv_cnts), F32)


def _zero_edge_row(a, period, first):
    n, c = a.shape
    a3 = a.reshape(n // period, period, c)
    row = lax.broadcasted_iota(jnp.int32, (1, SUBLANES, 1), 1)
    if first:
        band = jnp.where(row == 0, 0.0, a3[:, :SUBLANES])
        out = jnp.concatenate([band, a3[:, SUBLANES:]], 1)
    else:
        band = jnp.where(row == SUBLANES - 1, 0.0, a3[:, period - SUBLANES:])
        out = jnp.concatenate([a3[:, :period - SUBLANES], band], 1)
    return out.reshape(n, c)


def _dwconv3(x, w, period):
    nrows = x.shape[0]
    left = _zero_edge_row(pltpu.roll(x, 1, 0), period, True)
    right = _zero_edge_row(pltpu.roll(x, nrows - 1, 0), period, False)
    return w[0:1, :] * left + w[1:2, :] * x + w[2:3, :] * right


def _mix_out_kernel(x_ref, g_ref, vb_ref, ch_ref, of_ref, ob_ref, nw_ref, pw_ref, ps_ref, cw_ref,
                    band_ref, inv_cnt_ref, wout_ref, g1_ref, lng_ref, lnb_ref, o_ref, cat_ref,
                    *, mod_row, period, tm):
    g1 = g1_ref[mod_row:mod_row + 1, :]
    blocks = [slice(sb * MIX_ROWS, (sb + 1) * MIX_ROWS) for sb in range(tm // MIX_ROWS)]
    for rows in blocks:
        for h in range(HGRN_HEADS):
            cols = slice(h * HEAD_DIM, (h + 1) * HEAD_DIM)
            o = of_ref[rows, cols] + ob_ref[rows, cols]
            o = o * lax.rsqrt(jnp.mean(o * o, -1, keepdims=True) + EPS) * nw_ref[:, cols]
            cat_ref[rows, cols] = (o * _silu(g_ref[rows, cols])).astype(BF16)
        for gi in range(len(POOL_WINDOWS)):
            cols = slice(gi * POOL_GROUP, (gi + 1) * POOL_GROUP)
            v = vb_ref[rows, cols]
            v_hi, v_mid = _split2(v)
            band = band_ref[gi]
            dlt = (_dot(band, v_hi) + _dot(band, v_mid)) * inv_cnt_ref[gi] - v
            y = _dot(dlt.astype(BF16), pw_ref[gi]) * ps_ref[:, cols]
            cat_ref[rows, D_HGRN + gi * POOL_GROUP:D_HGRN + (gi + 1) * POOL_GROUP] = y.astype(BF16)
        bg = vb_ref[rows, D_POOL:]
        ch = ch_ref[rows, :D_CONV] * ch_ref[rows, D_CONV:]
        cat_ref[rows, D_HGRN + D_POOL:] = (bg * _dwconv3(ch, cw_ref[...], period)).astype(BF16)
    ys = [_dot(cat_ref[rows, :], wout_ref[...]) for rows in blocks]
    for rows, y in zip(blocks, ys):
        r = ALPHA * x_ref[rows, :] + g1 * y
        o_ref[rows, :] = _layer_norm(r) * lng_ref[...] + lnb_ref[...]


def _mix_out(x, p, o_f, o_b, mods, norm_w, pool_w, pool_scale, conv_w, pool_consts, w_out,
             ln_g, ln_b, *, layer, mod_row, period, tm):
    m, d = x.shape
    band, inv_cnt = pool_consts
    const = lambda a: pl.BlockSpec(a.shape, lambda i: (0,) * a.ndim)
    row = lambda width, cb: pl.BlockSpec((tm, width), lambda i: (i, cb))
    vec = lambda a: a.reshape(1, -1)
    args = [
        (x, row(d, 0)),
        (p, row(D_HGRN, 4)),
        (p, row(D_POOL + D_CONV, 5)),
        (p, row(2 * D_CONV, 6)),
        (o_f, row(D_HGRN, 0)),
        (o_b, row(D_HGRN, 0)),
        (vec(norm_w), None), (pool_w, None), (vec(pool_scale), None), (conv_w, None),
        (band, None), (inv_cnt, None),
        (w_out, pl.BlockSpec((None, d, d), lambda i: (layer, 0, 0))),
        (mods, pl.BlockSpec((ADA_ROWS, d), lambda i: (0, 2))),
        (vec(ln_g), None), (vec(ln_b), None),
    ]
    return pl.pallas_call(
        functools.partial(_mix_out_kernel, mod_row=mod_row, period=period, tm=tm),
        grid=(m // tm,),
        in_specs=[s if s is not None else const(a) for a, s in args],
        out_specs=pl.BlockSpec((tm, d), lambda i: (i, 0)),
        out_shape=jax.ShapeDtypeStruct((m, d), F32),
        scratch_shapes=[pltpu.VMEM((tm, d), BF16)],
        compiler_params=_cparams(("arbitrary",)),
        name="mix_out",
    )(*[a for a, _ in args])


def _ffn_kernel(x_ref, sh_ref, sc_ref, g2_ref, wu_ref, wg_ref, cw_ref, cb_ref, wd_ref,
                lng_ref, lnb_ref, o_ref, h_ref, *, mod_row, period, tm):
    j = pl.program_id(1)

    @pl.when(j == 0)
    def _():
        sh = sh_ref[mod_row:mod_row + 1, :]
        sc = sc_ref[mod_row:mod_row + 1, :]
        h_ref[...] = (_layer_norm(x_ref[...]) * (1.0 + sc) + sh).astype(BF16)
        o_ref[...] = jnp.zeros_like(o_ref)

    h = h_ref[...]
    width = wu_ref.shape[1] // FFN_SPLIT
    subs = [slice(a * width, (a + 1) * width) for a in range(FFN_SPLIT)]
    ups = [(_dot(h, wu_ref[:, cols]), _dot(h, wg_ref[:, cols])) for cols in subs]
    for cols, (u, g) in zip(subs, ups):
        hz = _dwconv3(g, 0.5 * cw_ref[:, cols], period) + 0.5 * cb_ref[:, cols]
        act = u * (hz + hz * jnp.tanh(hz))
        o_ref[...] += _dot(act.astype(BF16), wd_ref[cols, :])

    @pl.when(j == pl.num_programs(1) - 1)
    def _():
        r = ALPHA * x_ref[...] + g2_ref[mod_row:mod_row + 1, :] * o_ref[...]
        o_ref[...] = _layer_norm(r) * lng_ref[...] + lnb_ref[...]


def _ffn(x, mods, w_up, conv_w, conv_b, w_down, ln_g, ln_b, *, layer, mod_row, period, tm, tf):
    m, d = x.shape
    nf = D_FF // tf
    vec = lambda a: a.reshape(1, -1)
    mod = lambda cb: pl.BlockSpec((ADA_ROWS, d), lambda i, j: (0, cb))
    const = lambda a: pl.BlockSpec(a.shape, lambda i, j: (0,) * a.ndim)
    return pl.pallas_call(
        functools.partial(_ffn_kernel, mod_row=mod_row, period=period, tm=tm),
        grid=(m // tm, nf),
        in_specs=[
            pl.BlockSpec((tm, d), lambda i, j: (i, 0)),
            mod(3), mod(4), mod(5),
            pl.BlockSpec((None, d, tf), lambda i, j: (layer, 0, j)),
            pl.BlockSpec((None, d, tf), lambda i, j: (layer, 0, nf + j)),
            pl.BlockSpec((3, tf), lambda i, j: (0, j)),
            pl.BlockSpec((1, tf), lambda i, j: (0, j)),
            pl.BlockSpec((None, tf, d), lambda i, j: (layer, j, 0)),
            const(vec(ln_g)), const(vec(ln_b)),
        ],
        out_specs=pl.BlockSpec((tm, d), lambda i, j: (i, 0)),
        out_shape=jax.ShapeDtypeStruct((m, d), F32),
        scratch_shapes=[pltpu.VMEM((tm, d), BF16)],
        compiler_params=_cparams(("arbitrary", "arbitrary")),
        name="ffn",
    )(x, mods, mods, mods, w_up, w_up, conv_w, vec(conv_b), w_down, vec(ln_g), vec(ln_b))


def kernel(x, c, ctx, c_ctx, w_ada, b_ada, w_in, lb_logits, hgrn_norm_w, pool_w, pool_scale, conv_w,
           w_out, ln1_g, ln1_b, w_up, ffn_conv_w, ffn_conv_b, w_down, ln2_g, ln2_b):
    assert x.shape[0] == 1 and ctx.shape[0] == 1, "batch size 1 only"
    xs, cs = x[0], ctx[0]
    seq, ctx_len = xs.shape[0], cs.shape[0]

    cc = jnp.concatenate([c, c_ctx[None, :], jnp.zeros((ADA_ROWS - 2, D_MODEL), F32)], 0)
    mods_all = _ada(cc, w_ada, b_ada)

    gla_consts = _gla_constants()
    pool_x = _pool_constants(GRID_W)
    pool_c = _pool_constants(ctx_len)
    zero_state = jnp.zeros((HGRN_HEADS, HEAD_DIM, HEAD_DIM), F32)
    w_in_b, w_out_b = w_in.astype(BF16), w_out.astype(BF16)
    w_up_b, w_down_b, pool_w_b = w_up.astype(BF16), w_down.astype(BF16), pool_w.astype(BF16)

    for l in range(DEPTH):
        ctx_live = l < DEPTH - 1
        mods = mods_all[l]
        pc = _in_proj(cs, mods, w_in_b, layer=l, mod_row=1, tm=ctx_len, tn=1024)
        px = _in_proj(xs, mods, w_in_b, layer=l, mod_row=0, tm=1024, tn=1024)
        ocf, ocb, s_f, s_b = _gla(pc, lb_logits, zero_state, zero_state, gla_consts, layer=l, tb=ctx_len)
        oxf, oxb, _, _ = _gla(px, lb_logits, s_f, s_b, gla_consts, layer=l, tb=2048)
        mix = functools.partial(_mix_out, mods=mods, norm_w=hgrn_norm_w[l], pool_w=pool_w_b[l],
                                pool_scale=pool_scale[l], conv_w=conv_w[l], w_out=w_out_b,
                                ln_g=ln1_g[l], ln_b=ln1_b[l], layer=l)
        ffn = functools.partial(_ffn, mods=mods, w_up=w_up_b, conv_w=ffn_conv_w[l],
                                conv_b=ffn_conv_b[l], w_down=w_down_b, ln_g=ln2_g[l], ln_b=ln2_b[l],
                                layer=l, tf=512)
        xs = mix(xs, px, oxf, oxb, pool_consts=pool_x, mod_row=0, period=GRID_W, tm=MIX_ROWS)
        if ctx_live:
            cs = mix(cs, pc, ocf, ocb, pool_consts=pool_c, mod_row=1, period=ctx_len, tm=ctx_len)
        xs = ffn(xs, mod_row=0, period=GRID_W, tm=512)
        if ctx_live:
            cs = ffn(cs, mod_row=1, period=ctx_len, tm=ctx_len)
    return xs[None]
```

```python
import functools

import jax
import jax.numpy as jnp
import numpy as np
from jax import lax
from jax.experimental import pallas as pl
from jax.experimental.pallas import tpu as pltpu

D_MODEL = 2048
DEPTH = 2
GRID_W = 64
D_HGRN = 1024
HGRN_HEADS = 8
HEAD_DIM = D_HGRN // HGRN_HEADS
D_POOL = 512
POOL_WINDOWS = (2, 4, 8, 16)
POOL_GROUP = D_POOL // len(POOL_WINDOWS)
D_CONV = 512
D_FF = 5632
ALPHA = (2 * DEPTH) ** 0.25
EPS = 1e-6
F_MIN = 1e-30
F32 = jnp.float32
BF16 = jnp.bfloat16

GLA_CHUNK = 64
GLA_LEVELS = (32, 16, 8, 4, 2, 1)
GLA_UNROLL = 16
GLA_BLOCK = 2048
MIX_ROWS = 256
IN_PROJ_TILE = (1024, 1024)
FFN_TILE = (512, 512)
FFN_SPLIT = 2
ADA_COLS = 1024
SUBLANES = 8
ADA_ROWS = SUBLANES
VMEM_LIMIT = 56 * 1024 * 1024


def _cparams(sem):
    return pltpu.CompilerParams(dimension_semantics=sem, vmem_limit_bytes=VMEM_LIMIT)


def _silu(z):
    h = 0.5 * z
    return h + h * jnp.tanh(h)


def _layer_norm(x):
    mu = jnp.mean(x, -1, keepdims=True)
    xc = x - mu
    var = jnp.mean(xc * xc, -1, keepdims=True)
    return xc * lax.rsqrt(var + EPS)


def _dot_nt(a, b):
    return lax.dot_general(a, b, (((1,), (1,)), ((), ())), preferred_element_type=F32)


def _dot(a, b):
    return jnp.dot(a, b, preferred_element_type=F32)


def _neg_abs(a):
    bits = lax.bitcast_convert_type(a, jnp.uint32) | jnp.uint32(0x80000000)
    return lax.bitcast_convert_type(bits, F32)


def _split2(a):
    hi = a.astype(BF16)
    mid = (a - hi.astype(F32)).astype(BF16)
    return hi, mid


def _ada_kernel(cc_ref, w_ref, b_ref, o_ref):
    a = _silu(cc_ref[...])
    o_ref[...] = jnp.dot(a, w_ref[...], preferred_element_type=F32,
                         precision=lax.Precision.HIGHEST) + b_ref[...]


def _ada(cc, w_ada, b_ada):
    depth, d, n = w_ada.shape
    tn = ADA_COLS
    return pl.pallas_call(
        _ada_kernel,
        grid=(depth, n // tn),
        in_specs=[
            pl.BlockSpec((ADA_ROWS, d), lambda l, j: (0, 0)),
            pl.BlockSpec((None, d, tn), lambda l, j: (l, 0, j)),
            pl.BlockSpec((None, 1, tn), lambda l, j: (l, 0, j)),
        ],
        out_specs=pl.BlockSpec((None, ADA_ROWS, tn), lambda l, j: (l, 0, j)),
        out_shape=jax.ShapeDtypeStruct((depth, ADA_ROWS, n), F32),
        compiler_params=_cparams(("arbitrary", "arbitrary")),
        name="ada",
    )(cc, w_ada, b_ada.reshape(depth, 1, n))


def _in_proj_kernel(x_ref, sh_ref, sc_ref, w_ref, o_ref, h_ref, *, mod_row):
    @pl.when(pl.program_id(1) == 0)
    def _():
        sh = sh_ref[mod_row:mod_row + 1, :]
        sc = sc_ref[mod_row:mod_row + 1, :]
        h_ref[...] = (_layer_norm(x_ref[...]) * (1.0 + sc) + sh).astype(BF16)

    o_ref[...] = _dot(h_ref[...], w_ref[...])


def _in_proj(x, mods, w, *, layer, mod_row, tm, tn):
    m, d = x.shape
    n = w.shape[2]
    return pl.pallas_call(
        functools.partial(_in_proj_kernel, mod_row=mod_row),
        grid=(m // tm, n // tn),
        in_specs=[
            pl.BlockSpec((tm, d), lambda i, j: (i, 0)),
            pl.BlockSpec((ADA_ROWS, d), lambda i, j: (0, 0)),
            pl.BlockSpec((ADA_ROWS, d), lambda i, j: (0, 1)),
            pl.BlockSpec((None, d, tn), lambda i, j: (layer, 0, j)),
        ],
        out_specs=pl.BlockSpec((tm, tn), lambda i, j: (i, j)),
        out_shape=jax.ShapeDtypeStruct((m, n), F32),
        scratch_shapes=[pltpu.VMEM((tm, d), BF16)],
        compiler_params=_cparams(("arbitrary", "arbitrary")),
        name="in_proj",
    )(x, mods, mods, w)


def _gla_constants():
    n = GLA_CHUNK
    t = np.arange(n)[:, None]
    r = np.arange(n)[None, :]
    out = []
    for fwd in (True, False):
        masks = []
        for c in GLA_LEVELS:
            mid = (t // (2 * c)) * (2 * c) + c
            mid_s = (r // (2 * c)) * (2 * c) + c
            same = (t // (2 * c)) == (r // (2 * c))
            masks.append(same & ((t >= mid) & (r < mid_s) if fwd else (t < mid) & (r >= mid_s)))
        mid = (t // 4) * 4 + 2
        if fwd:
            cum = r <= t
            lvl2 = ((t >= mid) & (r >= mid) & (r <= t)) | ((t < mid) & (r > t) & (r <= mid - 1))
        else:
            cum = r >= t
            lvl2 = ((t >= mid) & (r >= mid) & (r <= t - 1)) | ((t < mid) & (r >= t) & (r <= mid - 1))
        w = np.concatenate([cum, lvl2], 0).astype(np.float32)
        out.append((jnp.asarray(w, BF16), jnp.asarray(np.stack(masks), F32)))
    return out


def _lower_bound(logits, layer):
    rows = [logits[j:j + 1, :] for j in range(DEPTH)]
    mx = functools.reduce(jnp.maximum, rows)
    es = [jnp.exp(r - mx) for r in rows]
    tot = functools.reduce(jnp.add, es)
    acc = jnp.zeros_like(mx)
    for j in range(1, layer + 1):
        acc = acc + es[j] / tot
    return acc


def _boundary_decay(b, c, fwd):
    bands, boundary = [], {}
    for band in range(GLA_CHUNK // SUBLANES):
        t0 = band * SUBLANES
        r = (t0 // (2 * c)) * 2 * c + (c - 1 if fwd else c)
        if r not in boundary:
            boundary[r] = jnp.broadcast_to(b[r:r + 1, :], (SUBLANES, b.shape[1]))
        rows = b[t0:t0 + SUBLANES]
        if c < SUBLANES:
            bands.append(_neg_abs(rows - boundary[r]))
        elif ((t0 // c) % 2 == 1) == fwd:
            bands.append(rows - boundary[r])
        else:
            bands.append(boundary[r] - rows)
    return jnp.concatenate(bands, 0)


def _gla_load(s):
    n = GLA_CHUNK
    s["rows"] = rows = pl.ds(pl.multiple_of(s["c"] * n, n), n)
    half = 0.5 * (1.0 - s["lb"])
    s["v"] = s["v_ref"][rows, :]
    s["q"] = _silu(s["q_ref"][rows, :])
    ht = half * jnp.tanh(0.5 * s["z_ref"][rows, :])
    s["f"] = jnp.maximum((s["lb"] + half) + ht, F_MIN)
    s["k"] = half - ht
    s["g2"] = _split2(jnp.log2(s["f"]))


def _gla_cumsum(s):
    w = s["w_ref"][...]
    s["d"] = _dot(w, s["g2"][0]) + _dot(w, s["g2"][1])


def _gla_decays(s):
    n, fwd = GLA_CHUNK, s["fwd"]
    b = s["d"][0:n]
    b_end = b[n - 1:n, :] if fwd else b[0:1, :]
    s["q_in"] = (s["q"] * jnp.exp2(b)).astype(BF16)
    s["k_out"] = (s["k"] * jnp.exp2(b_end - b)).astype(BF16)
    s["d_all"] = jnp.exp2(b_end)
    es = []
    for c in GLA_LEVELS:
        if c >= 4:
            es.append(jnp.exp2(_boundary_decay(b, c, fwd)))
        elif c == 2:
            es.append(jnp.exp2(s["d"][n:2 * n]))
        else:
            row = lax.broadcasted_iota(jnp.int32, (n, 1), 0)
            es.append(jnp.where((row & 1) == (1 if fwd else 0), s["f"], 1.0))
    s["es"] = [e.astype(BF16) for e in es]
    s["qk16"] = s["q"].astype(BF16), s["k"].astype(BF16)
    s["ps"] = []


def _gla_level(l, s):
    e = s["es"][l]
    q16, k16 = s["qk16"]
    s["ps"].append(_dot_nt(q16 * e, k16 * e))


def _level_selects_band(c, band, fwd):
    if c < SUBLANES:
        return True
    in_later_half = ((band * SUBLANES) // c) % 2 == 1
    return in_later_half if fwd else not in_later_half


def _gla_intra(s):
    bands = []
    for band in range(GLA_CHUNK // SUBLANES):
        rows = slice(band * SUBLANES, (band + 1) * SUBLANES)
        acc = None
        for l, c in enumerate(GLA_LEVELS):
            if _level_selects_band(c, band, s["fwd"]):
                p = s["m_ref"][l, rows, :] * s["ps"][l][rows, :]
                acc = p if acc is None else acc + p
        bands.append(acc)
    a = jnp.concatenate(bands, 0)
    o_self = jnp.sum(s["q"] * s["k"], -1, keepdims=True) * s["v"]
    s["o"] = o_self + _dot(a.astype(BF16), s["v"].astype(BF16))
    s["upd"] = lax.dot_general(s["v"].astype(BF16), s["k_out"], (((0,), (0,)), ((), ())),
                               preferred_element_type=F32)


def _gla_state(s):
    st = s["st"][0]
    s["o_ref"][s["rows"], :] = s["o"] + _dot(s["q_in"], st.T.astype(BF16))
    s["st"][0] = s["d_all"] * st + s["upd"]


_GLA_STAGES = ([_gla_load, _gla_cumsum, _gla_decays]
               + [functools.partial(_gla_level, l) for l in range(len(GLA_LEVELS))]
               + [_gla_intra, _gla_state])


def _gla_streams(streams):
    for stage in _GLA_STAGES:
        for s in streams:
            stage(s)


def _gla_kernel(qf_ref, vf_ref, zf_ref, qb_ref, vb_ref, zb_ref, lb_ref, s0f_ref, s0b_ref,
                wf_ref, mf_ref, wb_ref, mb_ref,
                of_ref, ob_ref, sf_ref, sb_ref, stf, stb, *, layer, nchunks):
    step = pl.program_id(1)

    @pl.when(step == 0)
    def _():
        stf[...] = s0f_ref[...]
        stb[...] = s0b_ref[...]

    lbf = _lower_bound(lb_ref[0], layer)
    lbb = _lower_bound(lb_ref[1], layer)

    unroll = min(GLA_UNROLL, nchunks)
    assert nchunks % unroll == 0

    def body(ci, carry):
        st_f, st_b = [stf[...]], [stb[...]]
        streams = []
        for u in range(unroll):
            c = ci * unroll + u
            streams.append(dict(q_ref=qf_ref, v_ref=vf_ref, z_ref=zf_ref, o_ref=of_ref, w_ref=wf_ref,
                                m_ref=mf_ref, lb=lbf, c=c, fwd=True, st=st_f))
            streams.append(dict(q_ref=qb_ref, v_ref=vb_ref, z_ref=zb_ref, o_ref=ob_ref, w_ref=wb_ref,
                                m_ref=mb_ref, lb=lbb, c=nchunks - 1 - c, fwd=False, st=st_b))
        _gla_streams(streams)
        stf[...] = st_f[0]
        stb[...] = st_b[0]
        return carry

    lax.fori_loop(0, nchunks // unroll, body, 0)

    @pl.when(step == pl.num_programs(1) - 1)
    def _():
        sf_ref[...] = stf[...]
        sb_ref[...] = stb[...]


def _gla(p, lb_logits, s0f, s0b, consts, *, layer, tb):
    t = p.shape[0]
    nb = t // tb
    hd, h = HEAD_DIM, HGRN_HEADS
    (wf, mf), (wb, mb) = consts

    def col(group, rev):
        if rev:
            return pl.BlockSpec((tb, hd), lambda hh, n: (nb - 1 - n, group * h + hh))
        return pl.BlockSpec((tb, hd), lambda hh, n: (n, group * h + hh))

    state_spec = pl.BlockSpec((None, hd, hd), lambda hh, n: (hh, 0, 0))
    const2 = lambda a: pl.BlockSpec(a.shape, lambda hh, n: (0,) * a.ndim)
    return pl.pallas_call(
        functools.partial(_gla_kernel, layer=layer, nchunks=tb // GLA_CHUNK),
        grid=(h, nb),
        in_specs=[
            col(0, False), col(1, False), col(2, False),
            col(0, True), col(1, True), col(3, True),
            pl.BlockSpec((2, DEPTH, hd), lambda hh, n: (0, 0, hh)),
            state_spec, state_spec,
            const2(wf), const2(mf), const2(wb), const2(mb),
        ],
        out_specs=[
            pl.BlockSpec((tb, hd), lambda hh, n: (n, hh)),
            pl.BlockSpec((tb, hd), lambda hh, n: (nb - 1 - n, hh)),
            state_spec, state_spec,
        ],
        out_shape=[
            jax.ShapeDtypeStruct((t, D_HGRN), F32),
            jax.ShapeDtypeStruct((t, D_HGRN), F32),
            jax.ShapeDtypeStruct((h, hd, hd), F32),
            jax.ShapeDtypeStruct((h, hd, hd), F32),
        ],
        scratch_shapes=[pltpu.VMEM((hd, hd), F32), pltpu.VMEM((hd, hd), F32)],
        compiler_params=_cparams(("arbitrary", "arbitrary")),
        name="gla",
    )(p, p, p, p, p, p, lb_logits, s0f, s0b, wf, mf, wb, mb)


def _pool_constants(period):
    n = MIX_ROWS
    t = np.arange(n)
    tau = t % period
    base = t - tau
    bands, inv_cnts = [], []
    for w in POOL_WINDOWS:
        lo = np.maximum(tau - w // 2, 0)
        hi = np.minimum(tau + w // 2, period)
        s = np.arange(n)[None, :]
        bands.append(((s >= (base + lo)[:, None]) & (s < (base + hi)[:, None])).astype(np.float32))
        inv_cnts.append(np.broadcast_to((1.0 / (hi - lo))[:, None], (n, POOL_GROUP)))
    return jnp.asarray(np.stack(bands), BF16), jnp.asarray(np.stack(inv_cnts), F32)


def _zero_edge_row(a, period, first):
    n, c = a.shape
    a3 = a.reshape(n // period, period, c)
    row = lax.broadcasted_iota(jnp.int32, (1, SUBLANES, 1), 1)
    if first:
        band = jnp.where(row == 0, 0.0, a3[:, :SUBLANES])
        out = jnp.concatenate([band, a3[:, SUBLANES:]], 1)
    else:
        band = jnp.where(row == SUBLANES - 1, 0.0, a3[:, period - SUBLANES:])
        out = jnp.concatenate([a3[:, :period - SUBLANES], band], 1)
    return out.reshape(n, c)


def _dwconv3(x, w, period):
    nrows = x.shape[0]
    left = _zero_edge_row(pltpu.roll(x, 1, 0), period, True)
    right = _zero_edge_row(pltpu.roll(x, nrows - 1, 0), period, False)
    return w[0:1, :] * left + w[1:2, :] * x + w[2:3, :] * right


def _mix_out_kernel(x_ref, g_ref, vb_ref, ch_ref, of_ref, ob_ref, nw_ref, pw_ref, ps_ref, cw_ref,
                    band_ref, inv_cnt_ref, wout_ref, g1_ref, lng_ref, lnb_ref, o_ref, cat_ref,
                    *, mod_row, period, tm):
    g1 = g1_ref[mod_row:mod_row + 1, :]
    blocks = [slice(sb * MIX_ROWS, (sb + 1) * MIX_ROWS) for sb in range(tm // MIX_ROWS)]
    for rows in blocks:
        for h in range(HGRN_HEADS):
            cols = slice(h * HEAD_DIM, (h + 1) * HEAD_DIM)
            o = of_ref[rows, cols] + ob_ref[rows, cols]
            o = o * lax.rsqrt(jnp.mean(o * o, -1, keepdims=True) + EPS) * nw_ref[:, cols]
            cat_ref[rows, cols] = (o * _silu(g_ref[rows, cols])).astype(BF16)
        for gi in range(len(POOL_WINDOWS)):
            cols = slice(gi * POOL_GROUP, (gi + 1) * POOL_GROUP)
            v = vb_ref[rows, cols]
            v_hi, v_mid = _split2(v)
            band = band_ref[gi]
            dlt = (_dot(band, v_hi) + _dot(band, v_mid)) * inv_cnt_ref[gi] - v
            y = _dot(dlt.astype(BF16), pw_ref[gi]) * ps_ref[:, cols]
            cat_ref[rows, D_HGRN + gi * POOL_GROUP:D_HGRN + (gi + 1) * POOL_GROUP] = y.astype(BF16)
        bg = vb_ref[rows, D_POOL:]
        ch = ch_ref[rows, :D_CONV] * ch_ref[rows, D_CONV:]
        cat_ref[rows, D_HGRN + D_POOL:] = (bg * _dwconv3(ch, cw_ref[...], period)).astype(BF16)
    ys = [_dot(cat_ref[rows, :], wout_ref[...]) for rows in blocks]
    for rows, y in zip(blocks, ys):
        r = ALPHA * x_ref[rows, :] + g1 * y
        o_ref[rows, :] = _layer_norm(r) * lng_ref[...] + lnb_ref[...]


def _mix_out(x, p, o_f, o_b, mods, norm_w, pool_w, pool_scale, conv_w, pool_consts, w_out,
             ln_g, ln_b, *, layer, mod_row, period, tm):
    m, d = x.shape
    band, inv_cnt = pool_consts
    const = lambda a: pl.BlockSpec(a.shape, lambda i: (0,) * a.ndim)
    row = lambda width, cb: pl.BlockSpec((tm, width), lambda i: (i, cb))
    vec = lambda a: a.reshape(1, -1)
    args = [
        (x, row(d, 0)),
        (p, row(D_HGRN, 4)),
        (p, row(D_POOL + D_CONV, 5)),
        (p, row(2 * D_CONV, 6)),
        (o_f, row(D_HGRN, 0)),
        (o_b, row(D_HGRN, 0)),
        (vec(norm_w), None), (pool_w, None), (vec(pool_scale), None), (conv_w, None),
        (band, None), (inv_cnt, None),
        (w_out, pl.BlockSpec((None, d, d), lambda i: (layer, 0, 0))),
        (mods, pl.BlockSpec((ADA_ROWS, d), lambda i: (0, 2))),
        (vec(ln_g), None), (vec(ln_b), None),
    ]
    return pl.pallas_call(
        functools.partial(_mix_out_kernel, mod_row=mod_row, period=period, tm=tm),
        grid=(m // tm,),
        in_specs=[s if s is not None else const(a) for a, s in args],
        out_specs=pl.BlockSpec((tm, d), lambda i: (i, 0)),
        out_shape=jax.ShapeDtypeStruct((m, d), F32),
        scratch_shapes=[pltpu.VMEM((tm, d), BF16)],
        compiler_params=_cparams(("arbitrary",)),
        name="mix_out",
    )(*[a for a, _ in args])


def _ffn_kernel(x_ref, sh_ref, sc_ref, g2_ref, wu_ref, wg_ref, cw_ref, cb_ref, wd_ref,
                lng_ref, lnb_ref, o_ref, h_ref, *, mod_row, period, tm):
    j = pl.program_id(1)

    @pl.when(j == 0)
    def _():
        sh = sh_ref[mod_row:mod_row + 1, :]
        sc = sc_ref[mod_row:mod_row + 1, :]
        h_ref[...] = (_layer_norm(x_ref[...]) * (1.0 + sc) + sh).astype(BF16)
        o_ref[...] = jnp.zeros_like(o_ref)

    h = h_ref[...]
    width = wu_ref.shape[1] // FFN_SPLIT
    subs = [slice(a * width, (a + 1) * width) for a in range(FFN_SPLIT)]
    ups = [(_dot(h, wu_ref[:, cols]), _dot(h, wg_ref[:, cols])) for cols in subs]
    for cols, (u, g) in zip(subs, ups):
        hz = _dwconv3(g, 0.5 * cw_ref[:, cols], period) + 0.5 * cb_ref[:, cols]
        act = u * (hz + hz * jnp.tanh(hz))
        o_ref[...] += _dot(act.astype(BF16), wd_ref[cols, :])

    @pl.when(j == pl.num_programs(1) - 1)
    def _():
        r = ALPHA * x_ref[...] + g2_ref[mod_row:mod_row + 1, :] * o_ref[...]
        o_ref[...] = _layer_norm(r) * lng_ref[...] + lnb_ref[...]


def _ffn(x, mods, w_up, conv_w, conv_b, w_down, ln_g, ln_b, *, layer, mod_row, period, tm, tf):
    m, d = x.shape
    nf = D_FF // tf
    vec = lambda a: a.reshape(1, -1)
    mod = lambda cb: pl.BlockSpec((ADA_ROWS, d), lambda i, j: (0, cb))
    const = lambda a: pl.BlockSpec(a.shape, lambda i, j: (0,) * a.ndim)
    return pl.pallas_call(
        functools.partial(_ffn_kernel, mod_row=mod_row, period=period, tm=tm),
        grid=(m // tm, nf),
        in_specs=[
            pl.BlockSpec((tm, d), lambda i, j: (i, 0)),
            mod(3), mod(4), mod(5),
            pl.BlockSpec((None, d, tf), lambda i, j: (layer, 0, j)),
            pl.BlockSpec((None, d, tf), lambda i, j: (layer, 0, nf + j)),
            pl.BlockSpec((3, tf), lambda i, j: (0, j)),
            pl.BlockSpec((1, tf), lambda i, j: (0, j)),
            pl.BlockSpec((None, tf, d), lambda i, j: (layer, j, 0)),
            const(vec(ln_g)), const(vec(ln_b)),
        ],
        out_specs=pl.BlockSpec((tm, d), lambda i, j: (i, 0)),
        out_shape=jax.ShapeDtypeStruct((m, d), F32),
        scratch_shapes=[pltpu.VMEM((tm, d), BF16)],
        compiler_params=_cparams(("arbitrary", "arbitrary")),
        name="ffn",
    )(x, mods, mods, mods, w_up, w_up, conv_w, vec(conv_b), w_down, vec(ln_g), vec(ln_b))


def kernel(x, c, ctx, c_ctx, w_ada, b_ada, w_in, lb_logits, hgrn_norm_w, pool_w, pool_scale, conv_w,
           w_out, ln1_g, ln1_b, w_up, ffn_conv_w, ffn_conv_b, w_down, ln2_g, ln2_b):
    assert x.shape[0] == 1 and ctx.shape[0] == 1, "batch size 1 only"
    xs, cs = x[0], ctx[0]
    ctx_len = cs.shape[0]
    assert ctx_len == MIX_ROWS, "the context prefix is one mix_out tile"

    cc = jnp.concatenate([c, c_ctx[None, :], jnp.zeros((ADA_ROWS - 2, D_MODEL), F32)], 0)
    mods_all = _ada(cc, w_ada, b_ada)

    gla_consts = _gla_constants()
    pool_x = _pool_constants(GRID_W)
    pool_c = _pool_constants(ctx_len)
    zero_state = jnp.zeros((HGRN_HEADS, HEAD_DIM, HEAD_DIM), F32)
    w_in_b, w_out_b = w_in.astype(BF16), w_out.astype(BF16)
    w_up_b, w_down_b, pool_w_b = w_up.astype(BF16), w_down.astype(BF16), pool_w.astype(BF16)

    for l in range(DEPTH):
        ctx_live = l < DEPTH - 1
        mods = mods_all[l]
        pc = _in_proj(cs, mods, w_in_b, layer=l, mod_row=1, tm=ctx_len, tn=IN_PROJ_TILE[1])
        px = _in_proj(xs, mods, w_in_b, layer=l, mod_row=0, tm=IN_PROJ_TILE[0], tn=IN_PROJ_TILE[1])
        ocf, ocb, s_f, s_b = _gla(pc, lb_logits, zero_state, zero_state, gla_consts, layer=l, tb=ctx_len)
        oxf, oxb, _, _ = _gla(px, lb_logits, s_f, s_b, gla_consts, layer=l, tb=GLA_BLOCK)
        mix = functools.partial(_mix_out, mods=mods, norm_w=hgrn_norm_w[l], pool_w=pool_w_b[l],
                                pool_scale=pool_scale[l], conv_w=conv_w[l], w_out=w_out_b,
                                ln_g=ln1_g[l], ln_b=ln1_b[l], layer=l)
        ffn = functools.partial(_ffn, mods=mods, w_up=w_up_b, conv_w=ffn_conv_w[l],
                                conv_b=ffn_conv_b[l], w_down=w_down_b, ln_g=ln2_g[l], ln_b=ln2_b[l],
                                layer=l, tf=FFN_TILE[1])
        xs = mix(xs, px, oxf, oxb, pool_consts=pool_x, mod_row=0, period=GRID_W, tm=MIX_ROWS)
        if ctx_live:
            cs = mix(cs, pc, ocf, ocb, pool_consts=pool_c, mod_row=1, period=ctx_len, tm=ctx_len)
        xs = ffn(xs, mod_row=0, period=GRID_W, tm=FFN_TILE[0])
        if ctx_live:
            cs = ffn(cs, mod_row=1, period=ctx_len, tm=ctx_len)
    return xs[None]
```

```python
import functools

import jax
import jax.numpy as jnp
import numpy as np
from jax import lax
from jax.experimental import pallas as pl
from jax.experimental.pallas import tpu as pltpu

D_MODEL = 2048
DEPTH = 2
GRID_W = 64
D_HGRN = 1024
HGRN_HEADS = 8
HEAD_DIM = D_HGRN // HGRN_HEADS
D_POOL = 512
POOL_WINDOWS = (2, 4, 8, 16)
POOL_GROUP = D_POOL // len(POOL_WINDOWS)
D_CONV = 512
D_FF = 5632
ALPHA = (2 * DEPTH) ** 0.25
EPS = 1e-6
F_MIN = 1e-30
F32 = jnp.float32
BF16 = jnp.bfloat16

GLA_CHUNK = 64
GLA_LEVELS = (32, 16, 8, 4, 2, 1)
GLA_UNROLL = 16
GLA_BLOCK = 2048
MIX_ROWS = 256
IN_PROJ_TILE = (1024, 1024)
FFN_TILE = (512, 512)
FFN_SPLIT = 2
ADA_COLS = 1024
SUBLANES = 8
ADA_ROWS = SUBLANES
VMEM_LIMIT = 56 * 1024 * 1024


def _cparams(sem):
    return pltpu.CompilerParams(dimension_semantics=sem, vmem_limit_bytes=VMEM_LIMIT)


def _silu(z):
    h = 0.5 * z
    return h + h * jnp.tanh(h)


def _layer_norm(x):
    mu = jnp.mean(x, -1, keepdims=True)
    xc = x - mu
    var = jnp.mean(xc * xc, -1, keepdims=True)
    return xc * lax.rsqrt(var + EPS)


def _dot_nt(a, b):
    return lax.dot_general(a, b, (((1,), (1,)), ((), ())), preferred_element_type=F32)


def _dot(a, b):
    return jnp.dot(a, b, preferred_element_type=F32)


def _neg_abs(a):
    bits = lax.bitcast_convert_type(a, jnp.uint32) | jnp.uint32(0x80000000)
    return lax.bitcast_convert_type(bits, F32)


def _split2(a):
    hi = a.astype(BF16)
    mid = (a - hi.astype(F32)).astype(BF16)
    return hi, mid


def _ada_kernel(cc_ref, w_ref, b_ref, o_ref):
    a = _silu(cc_ref[...])
    o_ref[...] = jnp.dot(a, w_ref[...], preferred_element_type=F32,
                         precision=lax.Precision.HIGHEST) + b_ref[...]


def _ada(cc, w_ada, b_ada):
    depth, d, n = w_ada.shape
    tn = ADA_COLS
    return pl.pallas_call(
        _ada_kernel,
        grid=(depth, n // tn),
        in_specs=[
            pl.BlockSpec((ADA_ROWS, d), lambda l, j: (0, 0)),
            pl.BlockSpec((None, d, tn), lambda l, j: (l, 0, j)),
            pl.BlockSpec((None, 1, tn), lambda l, j: (l, 0, j)),
        ],
        out_specs=pl.BlockSpec((None, ADA_ROWS, tn), lambda l, j: (l, 0, j)),
        out_shape=jax.ShapeDtypeStruct((depth, ADA_ROWS, n), F32),
        compiler_params=_cparams(("arbitrary", "arbitrary")),
        name="ada",
    )(cc, w_ada, b_ada.reshape(depth, 1, n))


def _in_proj_kernel(x_ref, sh_ref, sc_ref, w_ref, o_ref, h_ref, *, mod_row):
    @pl.when(pl.program_id(1) == 0)
    def _():
        sh = sh_ref[mod_row:mod_row + 1, :]
        sc = sc_ref[mod_row:mod_row + 1, :]
        h_ref[...] = (_layer_norm(x_ref[...]) * (1.0 + sc) + sh).astype(BF16)

    o_ref[...] = _dot(h_ref[...], w_ref[...])


def _in_proj(x, mods, w, *, layer, mod_row, tm, tn):
    m, d = x.shape
    n = w.shape[2]
    return pl.pallas_call(
        functools.partial(_in_proj_kernel, mod_row=mod_row),
        grid=(m // tm, n // tn),
        in_specs=[
            pl.BlockSpec((tm, d), lambda i, j: (i, 0)),
            pl.BlockSpec((ADA_ROWS, d), lambda i, j: (0, 0)),
            pl.BlockSpec((ADA_ROWS, d), lambda i, j: (0, 1)),
            pl.BlockSpec((None, d, tn), lambda i, j: (layer, 0, j)),
        ],
        out_specs=pl.BlockSpec((tm, tn), lambda i, j: (i, j)),
        out_shape=jax.ShapeDtypeStruct((m, n), F32),
        scratch_shapes=[pltpu.VMEM((tm, d), BF16)],
        compiler_params=_cparams(("arbitrary", "arbitrary")),
        name="in_proj",
    )(x, mods, mods, w)


def _gla_constants():
    n = GLA_CHUNK
    t = np.arange(n)[:, None]
    r = np.arange(n)[None, :]
    out = []
    for fwd in (True, False):
        masks = []
        for c in GLA_LEVELS:
            mid = (t // (2 * c)) * (2 * c) + c
            mid_s = (r // (2 * c)) * (2 * c) + c
            same = (t // (2 * c)) == (r // (2 * c))
            masks.append(same & ((t >= mid) & (r < mid_s) if fwd else (t < mid) & (r >= mid_s)))
        mid = (t // 4) * 4 + 2
        if fwd:
            cum = r <= t
            lvl2 = ((t >= mid) & (r >= mid) & (r <= t)) | ((t < mid) & (r > t) & (r <= mid - 1))
        else:
            cum = r >= t
            lvl2 = ((t >= mid) & (r >= mid) & (r <= t - 1)) | ((t < mid) & (r >= t) & (r <= mid - 1))
        w = np.concatenate([cum, lvl2], 0).astype(np.float32)
        out.append((jnp.asarray(w, BF16), jnp.asarray(np.stack(masks), F32)))
    return out


def _lower_bound(logits, layer):
    rows = [logits[j:j + 1, :] for j in range(DEPTH)]
    mx = functools.reduce(jnp.maximum, rows)
    es = [jnp.exp(r - mx) for r in rows]
    tot = functools.reduce(jnp.add, es)
    acc = jnp.zeros_like(mx)
    for j in range(1, layer + 1):
        acc = acc + es[j] / tot
    return acc


def _boundary_decay(b, c, fwd):
    bands, boundary = [], {}
    for band in range(GLA_CHUNK // SUBLANES):
        t0 = band * SUBLANES
        r = (t0 // (2 * c)) * 2 * c + (c - 1 if fwd else c)
        if r not in boundary:
            boundary[r] = jnp.broadcast_to(b[r:r + 1, :], (SUBLANES, b.shape[1]))
        rows = b[t0:t0 + SUBLANES]
        if c < SUBLANES:
            bands.append(_neg_abs(rows - boundary[r]))
        elif ((t0 // c) % 2 == 1) == fwd:
            bands.append(rows - boundary[r])
        else:
            bands.append(boundary[r] - rows)
    return jnp.concatenate(bands, 0)


def _gla_load(s):
    n = GLA_CHUNK
    s["rows"] = rows = pl.ds(pl.multiple_of(s["c"] * n, n), n)
    half = 0.5 * (1.0 - s["lb"])
    s["v"] = s["v_ref"][rows, :]
    s["q"] = _silu(s["q_ref"][rows, :])
    ht = half * jnp.tanh(0.5 * s["z_ref"][rows, :])
    s["f"] = jnp.maximum((s["lb"] + half) + ht, F_MIN)
    s["k"] = half - ht
    s["g2"] = _split2(jnp.log2(s["f"]))


def _gla_cumsum(s):
    w = s["w_ref"][...]
    s["d"] = _dot(w, s["g2"][0]) + _dot(w, s["g2"][1])


def _gla_decays(s):
    n, fwd = GLA_CHUNK, s["fwd"]
    b = s["d"][0:n]
    b_end = b[n - 1:n, :] if fwd else b[0:1, :]
    s["q_in"] = (s["q"] * jnp.exp2(b)).astype(BF16)
    s["k_out"] = (s["k"] * jnp.exp2(b_end - b)).astype(BF16)
    s["d_all"] = jnp.exp2(b_end)
    es = []
    for c in GLA_LEVELS:
        if c >= 4:
            es.append(jnp.exp2(_boundary_decay(b, c, fwd)))
        elif c == 2:
            es.append(jnp.exp2(s["d"][n:2 * n]))
        else:
            row = lax.broadcasted_iota(jnp.int32, (n, 1), 0)
            es.append(jnp.where((row & 1) == (1 if fwd else 0), s["f"], 1.0))
    s["es"] = [e.astype(BF16) for e in es]
    s["qk16"] = s["q"].astype(BF16), s["k"].astype(BF16)
    s["ps"] = []


def _gla_level(l, s):
    e = s["es"][l]
    q16, k16 = s["qk16"]
    s["ps"].append(_dot_nt(q16 * e, k16 * e))


def _level_selects_band(c, band, fwd):
    if c < SUBLANES:
        return True
    in_later_half = ((band * SUBLANES) // c) % 2 == 1
    return in_later_half if fwd else not in_later_half


def _gla_intra(s):
    bands = []
    for band in range(GLA_CHUNK // SUBLANES):
        rows = slice(band * SUBLANES, (band + 1) * SUBLANES)
        acc = None
        for l, c in enumerate(GLA_LEVELS):
            if _level_selects_band(c, band, s["fwd"]):
                p = s["m_ref"][l, rows, :] * s["ps"][l][rows, :]
                acc = p if acc is None else acc + p
        bands.append(acc)
    a = jnp.concatenate(bands, 0)
    o_self = jnp.sum(s["q"] * s["k"], -1, keepdims=True) * s["v"]
    s["o"] = o_self + _dot(a.astype(BF16), s["v"].astype(BF16))
    s["upd"] = lax.dot_general(s["v"].astype(BF16), s["k_out"], (((0,), (0,)), ((), ())),
                               preferred_element_type=F32)


def _gla_state(s):
    st = s["st"][0]
    s["o_ref"][s["rows"], :] = s["o"] + _dot(s["q_in"], st.T.astype(BF16))
    s["st"][0] = s["d_all"] * st + s["upd"]


_GLA_STAGES = ([_gla_load, _gla_cumsum, _gla_decays]
               + [functools.partial(_gla_level, l) for l in range(len(GLA_LEVELS))]
               + [_gla_intra, _gla_state])


def _gla_streams(streams):
    for stage in _GLA_STAGES:
        for s in streams:
            stage(s)


def _gla_kernel(qf_ref, vf_ref, zf_ref, qb_ref, vb_ref, zb_ref, lb_ref, s0f_ref, s0b_ref,
                wf_ref, mf_ref, wb_ref, mb_ref,
                of_ref, ob_ref, sf_ref, sb_ref, stf, stb, *, layer, nchunks):
    step = pl.program_id(1)

    @pl.when(step == 0)
    def _():
        stf[...] = s0f_ref[...]
        stb[...] = s0b_ref[...]

    lbf = _lower_bound(lb_ref[0], layer)
    lbb = _lower_bound(lb_ref[1], layer)

    unroll = min(GLA_UNROLL, nchunks)
    assert nchunks % unroll == 0

    def body(ci, carry):
        st_f, st_b = [stf[...]], [stb[...]]
        streams = []
        for u in range(unroll):
            c = ci * unroll + u
            streams.append(dict(q_ref=qf_ref, v_ref=vf_ref, z_ref=zf_ref, o_ref=of_ref, w_ref=wf_ref,
                                m_ref=mf_ref, lb=lbf, c=c, fwd=True, st=st_f))
            streams.append(dict(q_ref=qb_ref, v_ref=vb_ref, z_ref=zb_ref, o_ref=ob_ref, w_ref=wb_ref,
                                m_ref=mb_ref, lb=lbb, c=nchunks - 1 - c, fwd=False, st=st_b))
        _gla_streams(streams)
        stf[...] = st_f[0]
        stb[...] = st_b[0]
        return carry

    lax.fori_loop(0, nchunks // unroll, body, 0)

    @pl.when(step == pl.num_programs(1) - 1)
    def _():
        sf_ref[...] = stf[...]
        sb_ref[...] = stb[...]


def _gla(p, lb_logits, s0f, s0b, consts, *, layer, tb):
    t = p.shape[0]
    nb = t // tb
    hd, h = HEAD_DIM, HGRN_HEADS
    (wf, mf), (wb, mb) = consts

    def col(group, rev):
        if rev:
            return pl.BlockSpec((tb, hd), lambda hh, n: (nb - 1 - n, group * h + hh))
        return pl.BlockSpec((tb, hd), lambda hh, n: (n, group * h + hh))

    state_spec = pl.BlockSpec((None, hd, hd), lambda hh, n: (hh, 0, 0))
    const2 = lambda a: pl.BlockSpec(a.shape, lambda hh, n: (0,) * a.ndim)
    return pl.pallas_call(
        functools.partial(_gla_kernel, layer=layer, nchunks=tb // GLA_CHUNK),
        grid=(h, nb),
        in_specs=[
            col(0, False), col(1, False), col(2, False),
            col(0, True), col(1, True), col(3, True),
            pl.BlockSpec((2, DEPTH, hd), lambda hh, n: (0, 0, hh)),
            state_spec, state_spec,
            const2(wf), const2(mf), const2(wb), const2(mb),
        ],
        out_specs=[
            pl.BlockSpec((tb, hd), lambda hh, n: (n, hh)),
            pl.BlockSpec((tb, hd), lambda hh, n: (nb - 1 - n, hh)),
            state_spec, state_spec,
        ],
        out_shape=[
            jax.ShapeDtypeStruct((t, D_HGRN), F32),
            jax.ShapeDtypeStruct((t, D_HGRN), F32),
            jax.ShapeDtypeStruct((h, hd, hd), F32),
            jax.ShapeDtypeStruct((h, hd, hd), F32),
        ],
        scratch_shapes=[pltpu.VMEM((hd, hd), F32), pltpu.VMEM((hd, hd), F32)],
        compiler_params=_cparams(("arbitrary", "arbitrary")),
        name="gla",
    )(p, p, p, p, p, p, lb_logits, s0f, s0b, wf, mf, wb, mb)


def _pool_constants(period):
    n = MIX_ROWS
    t = np.arange(n)
    tau = t % period
    base = t - tau
    bands, inv_cnts = [], []
    for w in POOL_WINDOWS:
        lo = np.maximum(tau - w // 2, 0)
        hi = np.minimum(tau + w // 2, period)
        s = np.arange(n)[None, :]
        bands.append(((s >= (base + lo)[:, None]) & (s < (base + hi)[:, None])).astype(np.float32))
        inv_cnts.append(np.broadcast_to((1.0 / (hi - lo))[:, None], (n, POOL_GROUP)))
    return jnp.asarray(np.stack(bands), BF16), jnp.asarray(np.stack(inv_cnts), F32)


def _zero_edge_row(a, period, first):
    n, c = a.shape
    a3 = a.reshape(n // period, period, c)
    row = lax.broadcasted_iota(jnp.int32, (1, SUBLANES, 1), 1)
    if first:
        band = jnp.where(row == 0, 0.0, a3[:, :SUBLANES])
        out = jnp.concatenate([band, a3[:, SUBLANES:]], 1)
    else:
        band = jnp.where(row == SUBLANES - 1, 0.0, a3[:, period - SUBLANES:])
        out = jnp.concatenate([a3[:, :period - SUBLANES], band], 1)
    return out.reshape(n, c)


def _dwconv3(x, w, period):
    nrows = x.shape[0]
    left = _zero_edge_row(pltpu.roll(x, 1, 0), period, True)
    right = _zero_edge_row(pltpu.roll(x, nrows - 1, 0), period, False)
    return w[0:1, :] * left + w[1:2, :] * x + w[2:3, :] * right


def _mix_out_kernel(x_ref, g_ref, vb_ref, ch_ref, of_ref, ob_ref, nw_ref, pw_ref, ps_ref, cw_ref,
                    band_ref, inv_cnt_ref, wout_ref, g1_ref, lng_ref, lnb_ref, o_ref, cat_ref,
                    *, mod_row, period, tm):
    g1 = g1_ref[mod_row:mod_row + 1, :]
    blocks = [slice(sb * MIX_ROWS, (sb + 1) * MIX_ROWS) for sb in range(tm // MIX_ROWS)]
    for rows in blocks:
        for h in range(HGRN_HEADS):
            cols = slice(h * HEAD_DIM, (h + 1) * HEAD_DIM)
            o = of_ref[rows, cols] + ob_ref[rows, cols]
            o = o * lax.rsqrt(jnp.mean(o * o, -1, keepdims=True) + EPS) * nw_ref[:, cols]
            cat_ref[rows, cols] = (o * _silu(g_ref[rows, cols])).astype(BF16)
        for gi in range(len(POOL_WINDOWS)):
            cols = slice(gi * POOL_GROUP, (gi + 1) * POOL_GROUP)
            v = vb_ref[rows, cols]
            v_hi, v_mid = _split2(v)
            band = band_ref[gi]
            dlt = (_dot(band, v_hi) + _dot(band, v_mid)) * inv_cnt_ref[gi] - v
            y = _dot(dlt.astype(BF16), pw_ref[gi]) * ps_ref[:, cols]
            cat_ref[rows, D_HGRN + gi * POOL_GROUP:D_HGRN + (gi + 1) * POOL_GROUP] = y.astype(BF16)
        bg = vb_ref[rows, D_POOL:]
        ch = ch_ref[rows, :D_CONV] * ch_ref[rows, D_CONV:]
        cat_ref[rows, D_HGRN + D_POOL:] = (bg * _dwconv3(ch, cw_ref[...], period)).astype(BF16)
    ys = [_dot(cat_ref[rows, :], wout_ref[...]) for rows in blocks]
    for rows, y in zip(blocks, ys):
        r = ALPHA * x_ref[rows, :] + g1 * y
        o_ref[rows, :] = _layer_norm(r) * lng_ref[...] + lnb_ref[...]


def _mix_out(x, p, o_f, o_b, mods, norm_w, pool_w, pool_scale, conv_w, pool_consts, w_out,
             ln_g, ln_b, *, layer, mod_row, period, tm):
    m, d = x.shape
    band, inv_cnt = pool_consts
    const = lambda a: pl.BlockSpec(a.shape, lambda i: (0,) * a.ndim)
    row = lambda width, cb: pl.BlockSpec((tm, width), lambda i: (i, cb))
    vec = lambda a: a.reshape(1, -1)
    args = [
        (x, row(d, 0)),
        (p, row(D_HGRN, 4)),
        (p, row(D_POOL + D_CONV, 5)),
        (p, row(2 * D_CONV, 6)),
        (o_f, row(D_HGRN, 0)),
        (o_b, row(D_HGRN, 0)),
        (vec(norm_w), None), (pool_w, None), (vec(pool_scale), None), (conv_w, None),
        (band, None), (inv_cnt, None),
        (w_out, pl.BlockSpec((None, d, d), lambda i: (layer, 0, 0))),
        (mods, pl.BlockSpec((ADA_ROWS, d), lambda i: (0, 2))),
        (vec(ln_g), None), (vec(ln_b), None),
    ]
    return pl.pallas_call(
        functools.partial(_mix_out_kernel, mod_row=mod_row, period=period, tm=tm),
        grid=(m // tm,),
        in_specs=[s if s is not None else const(a) for a, s in args],
        out_specs=pl.BlockSpec((tm, d), lambda i: (i, 0)),
        out_shape=jax.ShapeDtypeStruct((m, d), F32),
        scratch_shapes=[pltpu.VMEM((tm, d), BF16)],
        compiler_params=_cparams(("arbitrary",)),
        name="mix_out",
    )(*[a for a, _ in args])


def _ffn_kernel(x_ref, sh_ref, sc_ref, g2_ref, wu_ref, wg_ref, cw_ref, cb_ref, wd_ref,
                lng_ref, lnb_ref, o_ref, h_ref, *, mod_row, period, tm):
    j = pl.program_id(1)

    @pl.when(j == 0)
    def _():
        sh = sh_ref[mod_row:mod_row + 1, :]
        sc = sc_ref[mod_row:mod_row + 1, :]
        h_ref[...] = (_layer_norm(x_ref[...]) * (1.0 + sc) + sh).astype(BF16)
        o_ref[...] = jnp.zeros_like(o_ref)

    h = h_ref[...]
    width = wu_ref.shape[1] // FFN_SPLIT
    subs = [slice(a * width, (a + 1) * width) for a in range(FFN_SPLIT)]
    ups = [(_dot(h, wu_ref[:, cols]), _dot(h, wg_ref[:, cols])) for cols in subs]
    acts = []
    for cols, (u, g) in zip(subs, ups):
        hz = _dwconv3(g, 0.5 * cw_ref[:, cols], period) + 0.5 * cb_ref[:, cols]
        acts.append((u * (hz + hz * jnp.tanh(hz))).astype(BF16))
    o_ref[...] += _dot(jnp.concatenate(acts, 1), wd_ref[...])

    @pl.when(j == pl.num_programs(1) - 1)
    def _():
        r = ALPHA * x_ref[...] + g2_ref[mod_row:mod_row + 1, :] * o_ref[...]
        o_ref[...] = _layer_norm(r) * lng_ref[...] + lnb_ref[...]


def _ffn(x, mods, w_up, conv_w, conv_b, w_down, ln_g, ln_b, *, layer, mod_row, period, tm, tf):
    m, d = x.shape
    nf = D_FF // tf
    vec = lambda a: a.reshape(1, -1)
    mod = lambda cb: pl.BlockSpec((ADA_ROWS, d), lambda i, j: (0, cb))
    const = lambda a: pl.BlockSpec(a.shape, lambda i, j: (0,) * a.ndim)
    return pl.pallas_call(
        functools.partial(_ffn_kernel, mod_row=mod_row, period=period, tm=tm),
        grid=(m // tm, nf),
        in_specs=[
            pl.BlockSpec((tm, d), lambda i, j: (i, 0)),
            mod(3), mod(4), mod(5),
            pl.BlockSpec((None, d, tf), lambda i, j: (layer, 0, j)),
            pl.BlockSpec((None, d, tf), lambda i, j: (layer, 0, nf + j)),
            pl.BlockSpec((3, tf), lambda i, j: (0, j)),
            pl.BlockSpec((1, tf), lambda i, j: (0, j)),
            pl.BlockSpec((None, tf, d), lambda i, j: (layer, j, 0)),
            const(vec(ln_g)), const(vec(ln_b)),
        ],
        out_specs=pl.BlockSpec((tm, d), lambda i, j: (i, 0)),
        out_shape=jax.ShapeDtypeStruct((m, d), F32),
        scratch_shapes=[pltpu.VMEM((tm, d), BF16)],
        compiler_params=_cparams(("arbitrary", "arbitrary")),
        name="ffn",
    )(x, mods, mods, mods, w_up, w_up, conv_w, vec(conv_b), w_down, vec(ln_g), vec(ln_b))


def kernel(x, c, ctx, c_ctx, w_ada, b_ada, w_in, lb_logits, hgrn_norm_w, pool_w, pool_scale, conv_w,
           w_out, ln1_g, ln1_b, w_up, ffn_conv_w, ffn_conv_b, w_down, ln2_g, ln2_b):
    assert x.shape[0] == 1 and ctx.shape[0] == 1, "batch size 1 only"
    xs, cs = x[0], ctx[0]
    ctx_len = cs.shape[0]
    assert ctx_len == MIX_ROWS, "the context prefix is one mix_out tile"

    cc = jnp.concatenate([c, c_ctx[None, :], jnp.zeros((ADA_ROWS - 2, D_MODEL), F32)], 0)
    mods_all = _ada(cc, w_ada, b_ada)

    gla_consts = _gla_constants()
    pool_x = _pool_constants(GRID_W)
    pool_c = _pool_constants(ctx_len)
    zero_state = jnp.zeros((HGRN_HEADS, HEAD_DIM, HEAD_DIM), F32)
    w_in_b, w_out_b = w_in.astype(BF16), w_out.astype(BF16)
    w_up_b, w_down_b, pool_w_b = w_up.astype(BF16), w_down.astype(BF16), pool_w.astype(BF16)

    for l in range(DEPTH):
        ctx_live = l < DEPTH - 1
        mods = mods_all[l]
        pc = _in_proj(cs, mods, w_in_b, layer=l, mod_row=1, tm=ctx_len, tn=IN_PROJ_TILE[1])
        px = _in_proj(xs, mods, w_in_b, layer=l, mod_row=0, tm=IN_PROJ_TILE[0], tn=IN_PROJ_TILE[1])
        ocf, ocb, s_f, s_b = _gla(pc, lb_logits, zero_state, zero_state, gla_consts, layer=l, tb=ctx_len)
        oxf, oxb, _, _ = _gla(px, lb_logits, s_f, s_b, gla_consts, layer=l, tb=GLA_BLOCK)
        mix = functools.partial(_mix_out, mods=mods, norm_w=hgrn_norm_w[l], pool_w=pool_w_b[l],
                                pool_scale=pool_scale[l], conv_w=conv_w[l], w_out=w_out_b,
                                ln_g=ln1_g[l], ln_b=ln1_b[l], layer=l)
        ffn = functools.partial(_ffn, mods=mods, w_up=w_up_b, conv_w=ffn_conv_w[l],
                                conv_b=ffn_conv_b[l], w_down=w_down_b, ln_g=ln2_g[l], ln_b=ln2_b[l],
                                layer=l, tf=FFN_TILE[1])
        xs = mix(xs, px, oxf, oxb, pool_consts=pool_x, mod_row=0, period=GRID_W, tm=MIX_ROWS)
        if ctx_live:
            cs = mix(cs, pc, ocf, ocb, pool_consts=pool_c, mod_row=1, period=ctx_len, tm=ctx_len)
        xs = ffn(xs, mod_row=0, period=GRID_W, tm=FFN_TILE[0])
        if ctx_live:
            cs = ffn(cs, mod_row=1, period=ctx_len, tm=ctx_len)
    return xs[None]
```
